```python
import jax, jax.numpy as jnp
from jax import lax
import numpy as np

D_MODEL = 1024
BATCH = 2
SEQ = 8192
DEPTH = 1

HEAD_DIM = 64
N_Q_HEADS = 8
N_KV_HEADS = 2
Q_PER_KV = N_Q_HEADS // N_KV_HEADS
ATTN_WIDTH = N_Q_HEADS * HEAD_DIM
KV_WIDTH = N_KV_HEADS * HEAD_DIM
WINDOW = 128
ATTN_BLOCK = WINDOW
ROPE_THETA = 10000.0
CONV_WIDTH = D_MODEL - ATTN_WIDTH
CONV_K = 3
IN_PROJ_WIDTH = ATTN_WIDTH + 2 * KV_WIDTH + 3 * CONV_WIDTH
SPLITS = (ATTN_WIDTH, ATTN_WIDTH + KV_WIDTH, ATTN_WIDTH + 2 * KV_WIDTH,
          ATTN_WIDTH + 2 * KV_WIDTH + CONV_WIDTH, ATTN_WIDTH + 2 * KV_WIDTH + 2 * CONV_WIDTH)
PEER_HEADS = 8
PEER_KEYS = 128
PEER_EXPERTS = PEER_KEYS * PEER_KEYS
PEER_TOPK = 16
PEER_QDIM = 256
PEER_HALF = PEER_QDIM // 2
PEER_CHUNK = 128
N_MOD = 6
EPS = 1e-6
NEG = -1e30

kernel_name = "hymba_swa_sink_shortconv_peer_adaln"


def rmsnorm(x, g):
    x32 = x.astype(jnp.float32)
    y = x32 * lax.rsqrt(jnp.mean(x32 * x32, axis=-1, keepdims=True) + EPS)
    return (y * g.astype(jnp.float32)).astype(x.dtype)


def modulate(h, shift, scale):
    return h * (1.0 + scale[:, None, :]) + shift[:, None, :]


def rope(t, cos, sin):
    t1, t2 = jnp.split(t.astype(jnp.float32), 2, axis=-1)
    out = jnp.concatenate([t1 * cos - t2 * sin, t2 * cos + t1 * sin], axis=-1)
    return out.astype(t.dtype)


def sliding_window_attention(q, k, v, sinks):
    B, S = q.shape[0], q.shape[1]
    nb = S // ATTN_BLOCK
    qb = q.reshape(B, nb, ATTN_BLOCK, N_KV_HEADS, Q_PER_KV, HEAD_DIM)

    def band(t):
        tb = t.reshape(B, nb, ATTN_BLOCK, N_KV_HEADS, HEAD_DIM)
        prev = jnp.pad(tb, ((0, 0), (1, 0), (0, 0), (0, 0), (0, 0)))[:, :-1]
        return jnp.concatenate([prev, tb], axis=2)

    kb, vb = band(k), band(v)
    s = jnp.einsum('bnqhgd,bnkhd->bnhgqk', qb, kb).astype(jnp.float32) * (HEAD_DIM ** -0.5)
    qi = jnp.arange(ATTN_BLOCK)[:, None]
    kj = jnp.arange(2 * ATTN_BLOCK)[None, :]
    dist = qi + ATTN_BLOCK - kj
    blk = jnp.arange(nb)[:, None, None]
    valid = (dist >= 0) & (dist < WINDOW) & (blk * ATTN_BLOCK - ATTN_BLOCK + kj >= 0)
    s = jnp.where(valid[None, :, None, None], s, NEG)
    sink = sinks.astype(jnp.float32).reshape(1, 1, N_KV_HEADS, Q_PER_KV, 1, 1)
    m = jnp.maximum(jnp.max(s, axis=-1, keepdims=True), sink)
    e = jnp.exp(s - m)
    p = e / (jnp.sum(e, axis=-1, keepdims=True) + jnp.exp(sink - m))
    o = jnp.einsum('bnhgqk,bnkhd->bnqhgd', p.astype(v.dtype), vb)
    return o.reshape(B, S, ATTN_WIDTH)


def short_conv(h, w):
    S = h.shape[1]
    hp = jnp.pad(h, ((0, 0), (CONV_K - 1, 0), (0, 0)))
    y = w[0] * hp[:, 0:S]
    for k in range(1, CONV_K):
        y = y + w[k] * hp[:, k:k + S]
    return y


def peer(h, wq, subkeys, u_tab, v_tab):
    B, S, D = h.shape
    T = B * S
    xt = h.reshape(T, D)
    q = (xt @ wq).reshape(T, PEER_HEADS, 2, PEER_HALF)
    sc = jnp.einsum('thpd,hpkd->thpk', q, subkeys).astype(jnp.float32)
    vals, ids = lax.top_k(sc, PEER_TOPK)
    cand = (vals[:, :, 0, :, None] + vals[:, :, 1, None, :]).reshape(T, PEER_HEADS, PEER_TOPK * PEER_TOPK)
    cid = (ids[:, :, 0, :, None] * PEER_KEYS + ids[:, :, 1, None, :]).reshape(T, PEER_HEADS, PEER_TOPK * PEER_TOPK)
    top, pos = lax.top_k(cand, PEER_TOPK)
    idx = jnp.take_along_axis(cid, pos, axis=-1)
    g = jax.nn.softmax(top, axis=-1).astype(h.dtype)
    nc = T // PEER_CHUNK

    def expert_chunk(args):
        xc, ic, gc = args
        u = jnp.take(u_tab, ic, axis=0)
        a = jnp.einsum('chkd,cd->chk', u, xc)
        wgt = gc * jax.nn.gelu(a)
        vv = jnp.take(v_tab, ic, axis=0)
        return jnp.einsum('chk,chkd->cd', wgt, vv)

    out = lax.map(expert_chunk, (xt.reshape(nc, PEER_CHUNK, D),
                                 idx.reshape(nc, PEER_CHUNK, PEER_HEADS, PEER_TOPK),
                                 g.reshape(nc, PEER_CHUNK, PEER_HEADS, PEER_TOPK)))
    return out.reshape(B, S, D)


def setup_inputs(seed: int = 0) -> dict:
    key = jax.random.key(seed)
    ks = jax.random.split(key, 20)
    f32 = jnp.float32
    D = D_MODEL

    def nrm(k, shape, scale):
        return jax.random.normal(k, shape, f32) * scale

    return {
        "x": nrm(ks[0], (BATCH, SEQ, D), 1.0),
        "c": nrm(ks[1], (BATCH, D), 1.0),
        "positions": jnp.broadcast_to(jnp.arange(SEQ, dtype=jnp.int32)[None, :], (BATCH, SEQ)),
        "ada_w": nrm(ks[2], (DEPTH, D, N_MOD * D), 0.5 * D ** -0.5),
        "ada_b": nrm(ks[3], (DEPTH, N_MOD * D), 0.02),
        "norm1_g": 1.0 + nrm(ks[4], (DEPTH, D), 0.02),
        "w_in": nrm(ks[5], (DEPTH, D, IN_PROJ_WIDTH), D ** -0.5),
        "q_norm_g": 1.0 + nrm(ks[6], (DEPTH, HEAD_DIM), 0.02),
        "k_norm_g": 1.0 + nrm(ks[7], (DEPTH, HEAD_DIM), 0.02),
        "sinks": nrm(ks[8], (DEPTH, N_Q_HEADS), 0.5),
        "conv_w": nrm(ks[9], (DEPTH, CONV_K, CONV_WIDTH), CONV_K ** -0.5),
        "attn_out_g": 1.0 + nrm(ks[10], (DEPTH, ATTN_WIDTH), 0.02),
        "conv_out_g": 1.0 + nrm(ks[11], (DEPTH, CONV_WIDTH), 0.02),
        "w_out": nrm(ks[12], (DEPTH, D, D), D ** -0.5),
        "norm2_g": 1.0 + nrm(ks[13], (DEPTH, D), 0.02),
        "peer_wq": nrm(ks[14], (DEPTH, D, PEER_HEADS * PEER_QDIM), D ** -0.5),
        "peer_subkeys": nrm(ks[15], (DEPTH, PEER_HEADS, 2, PEER_KEYS, PEER_HALF), PEER_HALF ** -0.5),
        "peer_u": nrm(ks[16], (DEPTH, PEER_EXPERTS, D), D ** -0.5),
        "peer_v": nrm(ks[17], (DEPTH, PEER_EXPERTS, D), (PEER_HEADS * PEER_TOPK) ** -0.5),
    }


def reference(x, c, positions, ada_w, ada_b, norm1_g, w_in, q_norm_g, k_norm_g, sinks, conv_w,
              attn_out_g, conv_out_g, w_out, norm2_g, peer_wq, peer_subkeys, peer_u, peer_v):
    B, S, _ = x.shape
    inv_freq = ROPE_THETA ** (-jnp.arange(0, HEAD_DIM, 2, dtype=jnp.float32) / HEAD_DIM)
    ang = positions.astype(jnp.float32)[..., None] * inv_freq
    cos = jnp.cos(ang)[:, :, None, :]
    sin = jnp.sin(ang)[:, :, None, :]
    c_act = jax.nn.silu(c)

    for l in range(DEPTH):
        mod = c_act @ ada_w[l] + ada_b[l]
        shift1, scale1, gate1, shift2, scale2, gate2 = jnp.split(mod, N_MOD, axis=-1)

        h = modulate(rmsnorm(x, norm1_g[l]), shift1, scale1)
        proj = h @ w_in[l]
        q, k, v, cb, cc, cu = jnp.split(proj, SPLITS, axis=-1)

        q = rope(rmsnorm(q.reshape(B, S, N_Q_HEADS, HEAD_DIM), q_norm_g[l]), cos, sin)
        k = rope(rmsnorm(k.reshape(B, S, N_KV_HEADS, HEAD_DIM), k_norm_g[l]), cos, sin)
        v = v.reshape(B, S, N_KV_HEADS, HEAD_DIM)
        attn = sliding_window_attention(q, k, v, sinks[l])

        conv = cb * short_conv(cc * cu, conv_w[l])

        merged = jnp.concatenate([rmsnorm(attn, attn_out_g[l]), rmsnorm(conv, conv_out_g[l])], axis=-1)
        x = x + gate1[:, None, :] * (merged @ w_out[l])

        h2 = modulate(rmsnorm(x, norm2_g[l]), shift2, scale2)
        x = x + gate2[:, None, :] * peer(h2, peer_wq[l], peer_subkeys[l], peer_u[l], peer_v[l])
    return x
```

```python
import functools

import numpy as np
import jax
import jax.numpy as jnp
from jax import lax
from jax.experimental import pallas as pl
from jax.experimental.pallas import tpu as pltpu

D_MODEL = 1024
HEAD_DIM = 64
HALF_DIM = HEAD_DIM // 2
N_Q_HEADS = 8
N_KV_HEADS = 2
ATTN_WIDTH = N_Q_HEADS * HEAD_DIM
KV_WIDTH = N_KV_HEADS * HEAD_DIM
WINDOW = 128
ROPE_THETA = 10000.0
CONV_WIDTH = D_MODEL - ATTN_WIDTH
CONV_K = 3
PEER_HEADS = 8
PEER_KEYS = 128
PEER_EXPERTS = PEER_KEYS * PEER_KEYS
PEER_TOPK = 16
PEER_QDIM = 256
N_MOD = 6
EPS = 1e-6
NEG = -1e30

LANES = 128
SUBLANES = 8
VMEM_LIMIT = 56 * 1024 * 1024

Q_OFF = 0
K_OFF = ATTN_WIDTH
V_OFF = K_OFF + 2 * KV_WIDTH
CB_OFF = V_OFF + 2 * KV_WIDTH
CC_OFF = CB_OFF + CONV_WIDTH
CU_OFF = CC_OFF + CONV_WIDTH
PROJ_WIDTH = CU_OFF + CONV_WIDTH

SEQ_TILE = 512
ROUTER_TILE = 512
TOK_TILE = 512
EXP_TILE = 1024

_NN = (((1,), (0,)), ((), ()))
_NT = (((1,), (1,)), ((), ()))

f32 = jnp.float32
bf16 = jnp.bfloat16


def _split(x):
    hi = x.astype(bf16)
    lo = (x - hi.astype(f32)).astype(bf16)
    return hi, lo


def _dg(a, b, dims=_NN):
    return lax.dot_general(a, b, dims, preferred_element_type=f32)


def _dot3(a, b, dims=_NN):
    return _dg(a[0], b[0], dims) + (_dg(a[0], b[1], dims) + _dg(a[1], b[0], dims))


def _rms(x):
    return x * lax.rsqrt(jnp.mean(x * x, axis=-1, keepdims=True) + EPS)


def _mod_kernel(c_ref, w_ref, b_ref, o_ref):
    c = c_ref[...]
    ca = c * (1.0 / (1.0 + jnp.exp(-c)))
    o_ref[...] = _dot3(_split(ca), _split(w_ref[...])) + b_ref[...]


def _mod_call(c_pad, ada_w, ada_b):
    nb = 1536
    n = ada_w.shape[1]
    return pl.pallas_call(
        _mod_kernel,
        grid=(n // nb,),
        in_specs=[
            pl.BlockSpec((SUBLANES, D_MODEL), lambda j: (0, 0)),
            pl.BlockSpec((D_MODEL, nb), lambda j: (0, j)),
            pl.BlockSpec((1, nb), lambda j: (0, j)),
        ],
        out_specs=pl.BlockSpec((SUBLANES, nb), lambda j: (0, j)),
        out_shape=jax.ShapeDtypeStruct((SUBLANES, n), f32),
        compiler_params=pltpu.CompilerParams(
            dimension_semantics=("arbitrary",), vmem_limit_bytes=VMEM_LIMIT),
        name="mod",
    )(c_pad, ada_w, ada_b)


def _mixer_kernel(sink_ref, x_ref, pos_ref, mod_ref, g1_ref, winh_ref, winl_ref, gq_ref, gk_ref,
                  freq_ref, sgn_ref, msame_ref, convw_ref, ga_ref, gc_ref, wouth_ref, woutl_ref,
                  g2_ref, x1_ref, h2t_ref, qbuf, kbuf, vbuf, abuf, ucarry):
    ts = x_ref.shape[0]
    s_idx = pl.program_id(1)
    x = x_ref[...]
    mod = mod_ref[...]
    shift1, scale1, gate1 = mod[0:1], mod[1:2], mod[2:3]
    shift2, scale2 = mod[3:4], mod[4:5]

    h = _rms(x) * g1_ref[...]
    hs = _split(h * (1.0 + scale1) + shift1)

    def proj(c0, c1):
        return _dot3(hs, (winh_ref[:, c0:c1], winl_ref[:, c0:c1]))

    ang = pos_ref[...] * freq_ref[...]
    cosf = jnp.cos(ang)
    sins = jnp.sin(ang) * sgn_ref[...]
    msame = msame_ref[...]

    def headnorm_rope(t, g):
        sq = _split(t * t)
        ss = _dg(sq[0], msame) + _dg(sq[1], msame)
        tn = t * lax.rsqrt(ss * (1.0 / HEAD_DIM) + EPS) * g
        return tn * cosf + pltpu.roll(tn, LANES // 2, 1) * sins

    @pl.when(s_idx == 0)
    def _():
        kbuf[0:WINDOW, :] = jnp.zeros((WINDOW, 2 * LANES), f32)
        vbuf[0:WINDOW, :] = jnp.zeros((WINDOW, 2 * LANES), f32)
        ucarry[...] = jnp.zeros(ucarry.shape, f32)

    for c in range(ATTN_WIDTH // LANES):
        sl = slice(c * LANES, (c + 1) * LANES)
        qbuf[:, sl] = headnorm_rope(proj(Q_OFF + c * LANES, Q_OFF + (c + 1) * LANES), gq_ref[:, sl])
    for g in range(N_KV_HEADS):
        sl = slice(g * LANES, (g + 1) * LANES)
        kbuf[WINDOW:WINDOW + ts, sl] = headnorm_rope(
            proj(K_OFF + g * LANES, K_OFF + (g + 1) * LANES), gk_ref[:, sl])
    vbuf[WINDOW:WINDOW + ts, :] = proj(V_OFF, V_OFF + 2 * LANES)

    qi = lax.broadcasted_iota(jnp.int32, (WINDOW, 2 * WINDOW), 0)
    kj = lax.broadcasted_iota(jnp.int32, (WINDOW, 2 * WINDOW), 1)
    band = (kj > qi) & (kj <= qi + WINDOW)
    lane = lax.broadcasted_iota(jnp.int32, (1, LANES), 1)

    def attn_block(n, carry):
        r0 = pl.multiple_of(n * WINDOW, WINDOW)
        qb = qbuf[pl.ds(r0, WINDOW), :]
        kb = kbuf[pl.ds(r0, 2 * WINDOW), :]
        vb = vbuf[pl.ds(r0, 2 * WINDOW), :]
        first = jnp.logical_and(s_idx == 0, n == 0)
        valid = band & (kj >= jnp.where(first, WINDOW, 0))
        for c in range(ATTN_WIDTH // LANES):
            g = c // 2
            qt = qb[:, c * LANES:(c + 1) * LANES]
            kt = _split(kb[:, g * LANES:(g + 1) * LANES])
            vt = vb[:, g * LANES:(g + 1) * LANES]
            o = None
            for sub in range(2):
                qm = _split(jnp.where((lane // HALF_DIM) % 2 == sub, qt, 0.0))
                s = _dot3(qm, kt, _NT) * (HEAD_DIM ** -0.5)
                s = jnp.where(valid, s, NEG)
                sink = sink_ref[2 * c + sub]
                m = jnp.maximum(jnp.max(s, axis=-1, keepdims=True), sink)
                e = jnp.exp(s - m)
                den = jnp.sum(e, axis=-1, keepdims=True) + jnp.exp(sink - m)
                p = e / den
                vm = jnp.where(lane // HEAD_DIM == sub, vt, 0.0)
                od = _dot3(_split(p), _split(vm))
                o = od if o is None else o + od
            abuf[pl.ds(r0, WINDOW), c * LANES:(c + 1) * LANES] = o
        return carry

    lax.fori_loop(0, ts // WINDOW, attn_block, 0)
    kbuf[0:WINDOW, :] = kbuf[ts:ts + WINDOW, :]
    vbuf[0:WINDOW, :] = vbuf[ts:ts + WINDOW, :]

    cb = proj(CB_OFF, CB_OFF + CONV_WIDTH)
    u = proj(CC_OFF, CC_OFF + CONV_WIDTH) * proj(CU_OFF, CU_OFF + CONV_WIDTH)
    rowi = lax.broadcasted_iota(jnp.int32, u.shape, 0)
    prev1 = ucarry[SUBLANES - 1:SUBLANES, :]
    prev2 = ucarry[SUBLANES - 2:SUBLANES - 1, :]
    u1 = jnp.where(rowi == 0, prev1, pltpu.roll(u, 1, 0))
    u2 = jnp.where(rowi == 0, prev2, jnp.where(rowi == 1, prev1, pltpu.roll(u, 2, 0)))
    ucarry[...] = u[ts - SUBLANES:ts, :]
    w = convw_ref[...]
    conv = cb * (w[0:1] * u2 + w[1:2] * u1 + w[2:3] * u)

    ra = _split(_rms(abuf[...]) * ga_ref[...])
    rc = _split(_rms(conv) * gc_ref[...])
    y = (_dot3(ra, (wouth_ref[0:ATTN_WIDTH, :], woutl_ref[0:ATTN_WIDTH, :]))
         + _dot3(rc, (wouth_ref[ATTN_WIDTH:, :], woutl_ref[ATTN_WIDTH:, :])))
    x1 = x + gate1 * y
    x1_ref[...] = x1
    h2 = _rms(x1) * g2_ref[...]
    h2t_ref[...] = (h2 * (1.0 + scale2) + shift2).T


def _const_spec(shape):
    nd = len(shape)
    return pl.BlockSpec(shape, lambda *_: (0,) * nd, pipeline_mode=pl.Buffered(1))


def _mixer_call(sinks, x, pos, mod, g1, winh, winl, gq, gk, freq, sgn, msame, convw, ga, gc,
                wouth, woutl, g2):
    B, S, D = x.shape
    ts = SEQ_TILE
    nst = S // ts
    return pl.pallas_call(
        _mixer_kernel,
        grid=(B, nst),
        in_specs=[
            pl.BlockSpec(memory_space=pltpu.SMEM),
            pl.BlockSpec((None, ts, D), lambda b, s: (b, s, 0)),
            pl.BlockSpec((None, ts, 1), lambda b, s: (b, s, 0)),
            pl.BlockSpec((None, N_MOD, D), lambda b, s: (b, 0, 0)),
            _const_spec(g1.shape), _const_spec(winh.shape), _const_spec(winl.shape),
            _const_spec(gq.shape), _const_spec(gk.shape), _const_spec(freq.shape),
            _const_spec(sgn.shape), _const_spec(msame.shape), _const_spec(convw.shape),
            _const_spec(ga.shape), _const_spec(gc.shape), _const_spec(wouth.shape),
            _const_spec(woutl.shape), _const_spec(g2.shape),
        ],
        out_specs=[
            pl.BlockSpec((None, ts, D), lambda b, s: (b, s, 0)),
            pl.BlockSpec((D, ts), lambda b, s: (0, b * nst + s)),
        ],
        out_shape=[
            jax.ShapeDtypeStruct((B, S, D), f32),
            jax.ShapeDtypeStruct((D, B * S), f32),
        ],
        scratch_shapes=[
            pltpu.VMEM((ts, ATTN_WIDTH), f32),
            pltpu.VMEM((WINDOW + ts, 2 * LANES), f32),
            pltpu.VMEM((WINDOW + ts, 2 * LANES), f32),
            pltpu.VMEM((ts, ATTN_WIDTH), f32),
            pltpu.VMEM((SUBLANES, CONV_WIDTH), f32),
        ],
        compiler_params=pltpu.CompilerParams(
            dimension_semantics=("arbitrary", "arbitrary"), vmem_limit_bytes=VMEM_LIMIT),
        name="mixer",
    )(sinks, x, pos, mod, g1, winh, winl, gq, gk, freq, sgn, msame, convw, ga, gc, wouth, woutl, g2)


def _oddeven_merge_sort_pairs(n):
    pairs = []
    p = 1
    while p < n:
        k = p
        while k >= 1:
            for j in range(k % p, n - k, 2 * k):
                for i in range(min(k, n - j - k)):
                    if (i + j) // (2 * p) == (i + j + k) // (2 * p):
                        pairs.append((i + j, i + j + k))
            k //= 2
        p *= 2
    return pairs


_SORT16 = _oddeven_merge_sort_pairs(PEER_TOPK)


def _bitonic_desc(z):
    z = list(z)
    d = PEER_TOPK // 2
    while d >= 1:
        for r in range(PEER_TOPK):
            if not r & d:
                hi, lo = jnp.maximum(z[r], z[r + d]), jnp.minimum(z[r], z[r + d])
                z[r], z[r + d] = hi, lo
        d //= 2
    return z


def _merge_top(R, L):
    z = list(R)
    for r in range(PEER_TOPK - len(L), PEER_TOPK):
        z[r] = jnp.maximum(R[r], L[PEER_TOPK - 1 - r])
    return _bitonic_desc(z)


def _top16_rows(sc):
    x = [sc[SUBLANES * g:SUBLANES * (g + 1), :] for g in range(PEER_KEYS // SUBLANES)]
    for i, j in _SORT16:
        x[i], x[j] = jnp.maximum(x[i], x[j]), jnp.minimum(x[i], x[j])
    for shift in (4, 2, 1):
        z = [jnp.maximum(x[r], pltpu.roll(x[PEER_TOPK - 1 - r], shift, 0)) for r in range(PEER_TOPK)]
        x = _bitonic_desc(z)
    return x


_STAIR = [(r1, r2) for r1 in range(PEER_TOPK) for r2 in range(PEER_TOPK)
          if (r1 + 1) * (r2 + 1) <= PEER_TOPK]


def _router_kernel(h2t_ref, wqh_ref, wql_ref, skh_ref, skl_ref,
                   f1_ref, f2_ref, kap_ref, h2b_ref, q_s, sc_s, top_s, row_s):
    tr = h2t_ref.shape[1]
    hs = _split(h2t_ref[...])
    h2b_ref[...] = hs[0]
    q_s[...] = _dot3((wqh_ref[...], wql_ref[...]), hs)
    top_s[...] = jnp.zeros(top_s.shape, f32)
    sub = lax.broadcasted_iota(jnp.int32, (SUBLANES, tr), 0)

    def head_body(h, carry):
        for p in range(2):
            hp = 2 * h + p
            r0 = pl.multiple_of(hp * PEER_KEYS, PEER_KEYS)
            sc = _dot3((skh_ref[hp], skl_ref[hp]), _split(q_s[pl.ds(r0, PEER_KEYS), :]))
            sc_s[hp] = sc
            top = _top16_rows(sc)
            for r in range(PEER_TOPK):
                top_s[p, r] = jnp.where(sub == h, top[r], top_s[p, r])
        return carry

    lax.fori_loop(0, PEER_HEADS, head_body, 0)

    a = [top_s[0, r] for r in range(PEER_TOPK)]
    b = [top_s[1, r] for r in range(PEER_TOPK)]
    R = [a[0] + b[r] for r in range(16)]
    R = _merge_top(R, [a[r] + b[0] for r in range(1, 16)])
    R = _merge_top(R, [a[1] + b[r] for r in range(1, 8)])
    R = _merge_top(R, [a[r] + b[1] for r in range(2, 8)])
    R = _merge_top(R, [a[2] + b[r] for r in range(2, 5)])
    R = _merge_top(R, [a[r] + b[2] for r in range(3, 5)])
    R = _merge_top(R, [a[3] + b[3]])
    tau = R[PEER_TOPK - 1]
    z = jnp.exp(R[0] - R[0])
    for r in range(1, PEER_TOPK):
        z = z + jnp.exp(R[r] - R[0])
    inv_z = 1.0 / z
    fa = [jnp.exp(a[r] - a[0]) * inv_z for r in range(PEER_TOPK)]
    fb = [jnp.exp(b[r] - b[0]) for r in range(PEER_TOPK)]
    kap = jnp.full((SUBLANES, tr), jnp.inf, f32)
    for r1, r2 in _STAIR:
        kap = jnp.minimum(kap, jnp.where(a[r1] + b[r2] >= tau, fa[r1] * fb[r2], jnp.inf))
    kap_ref[...] = kap
    row_s[0] = a[0]
    row_s[1] = b[0]
    row_s[2] = inv_z

    def fac_body(h, carry):
        a0 = row_s[0, pl.ds(h, 1), :]
        b0 = row_s[1, pl.ds(h, 1), :]
        iz = row_s[2, pl.ds(h, 1), :]
        f1_ref[h] = jnp.exp(sc_s[2 * h] - a0) * iz
        f2_ref[h] = jnp.exp(sc_s[2 * h + 1] - b0)
        return carry

    lax.fori_loop(0, PEER_HEADS, fac_body, 0)


def _router_call(h2t, wqh, wql, skh, skl):
    D, T = h2t.shape
    tr = ROUTER_TILE
    return pl.pallas_call(
        _router_kernel,
        grid=(T // tr,),
        in_specs=[
            pl.BlockSpec((D, tr), lambda i: (0, i)),
            _const_spec(wqh.shape), _const_spec(wql.shape),
            _const_spec(skh.shape), _const_spec(skl.shape),
        ],
        out_specs=[
            pl.BlockSpec((PEER_HEADS, PEER_KEYS, tr), lambda i: (0, 0, i)),
            pl.BlockSpec((PEER_HEADS, PEER_KEYS, tr), lambda i: (0, 0, i)),
            pl.BlockSpec((PEER_HEADS, tr), lambda i: (0, i)),
            pl.BlockSpec((D, tr), lambda i: (0, i)),
        ],
        out_shape=[
            jax.ShapeDtypeStruct((PEER_HEADS, PEER_KEYS, T), f32),
            jax.ShapeDtypeStruct((PEER_HEADS, PEER_KEYS, T), f32),
            jax.ShapeDtypeStruct((PEER_HEADS, T), f32),
            jax.ShapeDtypeStruct((D, T), bf16),
        ],
        scratch_shapes=[
            pltpu.VMEM((PEER_HEADS * PEER_QDIM, tr), f32),
            pltpu.VMEM((2 * PEER_HEADS, PEER_KEYS, tr), f32),
            pltpu.VMEM((2, PEER_TOPK, SUBLANES, tr), f32),
            pltpu.VMEM((3, SUBLANES, tr), f32),
        ],
        compiler_params=pltpu.CompilerParams(
            dimension_semantics=("arbitrary",), vmem_limit_bytes=VMEM_LIMIT),
        name="router",
    )(h2t, wqh, wql, skh, skl)


_GELU_C1 = float(np.sqrt(2.0 / np.pi))
_GELU_C2 = 0.044715 * _GELU_C1


def _experts_kernel(h2b_ref, f1_ref, f2_ref, kap_ref, u_ref, vt_ref, x1_ref, mod_ref,
                    out_ref, acc, a_s, w_s):
    tm = h2b_ref.shape[1]
    te = u_ref.shape[0]
    j = pl.program_id(1)

    @pl.when(j == 0)
    def _():
        acc[...] = jnp.zeros(acc.shape, f32)

    a_s[...] = _dg(u_ref[...], h2b_ref[...])

    assert te == SUBLANES * PEER_KEYS
    i0 = pl.multiple_of(j * SUBLANES, SUBLANES)
    for r in range(SUBLANES):
        rows = slice(r * PEER_KEYS, (r + 1) * PEER_KEYS)

        def lane_tile(tc, carry):
            cols = pl.ds(pl.multiple_of(tc * LANES, LANES), LANES)
            g = jnp.zeros((PEER_KEYS, LANES), f32)
            for h in range(PEER_HEADS):
                f1 = f1_ref[h, pl.ds(i0, SUBLANES), cols][r:r + 1, :]
                p = f1 * f2_ref[h, :, cols]
                g = g + jnp.where(p >= kap_ref[h:h + 1, cols], p, 0.0)
            a = a_s[rows, cols]
            t = jnp.tanh(a * (_GELU_C1 + _GELU_C2 * (a * a)))
            w_s[rows, cols] = ((0.5 * g) * (a * (1.0 + t))).astype(bf16)
            return carry

        lax.fori_loop(0, tm // LANES, lane_tile, 0)

    acc[...] += _dg(vt_ref[...], w_s[...])

    @pl.when(j == pl.num_programs(1) - 1)
    def _():
        gate2 = mod_ref[N_MOD - 1:N_MOD, :]
        out_ref[...] = x1_ref[...] + gate2 * acc[...].T


def _experts_call(h2b, f1, f2, kap, u_b, vt_b, x1, mod, tiles_per_batch):
    D, T = h2b.shape
    tm, te = TOK_TILE, EXP_TILE
    return pl.pallas_call(
        _experts_kernel,
        grid=(T // tm, PEER_EXPERTS // te),
        in_specs=[
            pl.BlockSpec((D, tm), lambda i, j: (0, i)),
            pl.BlockSpec((PEER_HEADS, PEER_KEYS, tm), lambda i, j: (0, 0, i)),
            pl.BlockSpec((PEER_HEADS, PEER_KEYS, tm), lambda i, j: (0, 0, i)),
            pl.BlockSpec((PEER_HEADS, tm), lambda i, j: (0, i)),
            pl.BlockSpec((te, D), lambda i, j: (j, 0)),
            pl.BlockSpec((D, te), lambda i, j: (0, j)),
            pl.BlockSpec((tm, D), lambda i, j: (i, 0)),
            pl.BlockSpec((None, N_MOD, D), lambda i, j: (i // tiles_per_batch, 0, 0)),
        ],
        out_specs=pl.BlockSpec((tm, D), lambda i, j: (i, 0)),
        out_shape=jax.ShapeDtypeStruct((T, D), f32),
        scratch_shapes=[
            pltpu.VMEM((D, tm), f32),
            pltpu.VMEM((te, tm), f32),
            pltpu.VMEM((te, tm), bf16),
        ],
        compiler_params=pltpu.CompilerParams(
            dimension_semantics=("arbitrary", "arbitrary"), vmem_limit_bytes=VMEM_LIMIT),
        name="experts",
    )(h2b, f1, f2, kap, u_b, vt_b, x1, mod)


def _proj_columns():
    half = np.arange(HALF_DIM)
    cols = []
    for c in range(N_Q_HEADS // 2):
        for part in range(2):
            for head in (2 * c, 2 * c + 1):
                cols.append(head * HEAD_DIM + part * HALF_DIM + half)
    for g in range(N_KV_HEADS):
        for part in (0, 0, 1, 1):
            cols.append(ATTN_WIDTH + g * HEAD_DIM + part * HALF_DIM + half)
    for g in range(N_KV_HEADS):
        for _ in range(2):
            cols.append(ATTN_WIDTH + KV_WIDTH + g * HEAD_DIM + np.arange(HEAD_DIM))
    cols.append(np.arange(ATTN_WIDTH + 2 * KV_WIDTH, ATTN_WIDTH + 2 * KV_WIDTH + 3 * CONV_WIDTH))
    return np.concatenate(cols)


_PROJ_COLS = _proj_columns()
_TILE_DIM = np.concatenate([np.arange(HALF_DIM), np.arange(HALF_DIM),
                            HALF_DIM + np.arange(HALF_DIM), HALF_DIM + np.arange(HALF_DIM)])
_TILE_HEAD = (np.arange(LANES) // HALF_DIM) % 2
_MSAME = (_TILE_HEAD[:, None] == _TILE_HEAD[None, :]).astype(np.float32)
_INV_FREQ = (ROPE_THETA ** (-np.arange(0, HEAD_DIM, 2, dtype=np.float32) / HEAD_DIM)).astype(np.float32)


def kernel(x, c, positions, ada_w, ada_b, norm1_g, w_in, q_norm_g, k_norm_g, sinks, conv_w,
           attn_out_g, conv_out_g, w_out, norm2_g, peer_wq, peer_subkeys, peer_u, peer_v):
    B, S, D = x.shape
    T = B * S
    assert ada_w.shape[0] == 1, "single-layer block"
    assert D == D_MODEL and S % SEQ_TILE == 0 and T % TOK_TILE == 0 and T % ROUTER_TILE == 0
    assert S % TOK_TILE == 0

    c_pad = jnp.pad(c, ((0, SUBLANES - B), (0, 0)))
    mod = _mod_call(c_pad, ada_w[0], ada_b[0][None, :])[:B].reshape(B, N_MOD, D)

    inv_freq = ROPE_THETA ** (-jnp.arange(0, HEAD_DIM, 2, dtype=f32) / HEAD_DIM)
    freq = inv_freq[_TILE_DIM % HALF_DIM][None, :]
    sgn = jnp.asarray(np.where(np.arange(LANES) < LANES // 2, -1.0, 1.0), f32)[None, :]
    winh, winl = _split(w_in[0][:, _PROJ_COLS])
    wouth, woutl = _split(w_out[0])
    gq = jnp.tile(q_norm_g[0][_TILE_DIM], ATTN_WIDTH // LANES)[None, :]
    gk = jnp.tile(k_norm_g[0][_TILE_DIM], N_KV_HEADS)[None, :]
    x1, h2t = _mixer_call(
        sinks[0], x, positions.astype(f32)[..., None], mod, norm1_g, winh, winl, gq, gk, freq, sgn,
        jnp.asarray(_MSAME, bf16), conv_w[0], attn_out_g, conv_out_g, wouth, woutl, norm2_g)

    wqh, wql = _split(peer_wq[0].T)
    skh, skl = _split(peer_subkeys[0].reshape(2 * PEER_HEADS, PEER_KEYS, PEER_QDIM // 2))
    f1, f2, kap, h2b = _router_call(h2t, wqh, wql, skh, skl)

    out = _experts_call(h2b, f1, f2, kap, peer_u[0].astype(bf16), peer_v[0].T.astype(bf16),
                        x1.reshape(T, D), mod, S // TOK_TILE)
    return out.reshape(B, S, D)
```

```python
import functools

import numpy as np
import jax
import jax.numpy as jnp
from jax import lax
from jax.experimental import pallas as pl
from jax.experimental.pallas import tpu as pltpu

D_MODEL = 1024
HEAD_DIM = 64
HALF_DIM = HEAD_DIM // 2
N_Q_HEADS = 8
N_KV_HEADS = 2
ATTN_WIDTH = N_Q_HEADS * HEAD_DIM
KV_WIDTH = N_KV_HEADS * HEAD_DIM
WINDOW = 128
ROPE_THETA = 10000.0
CONV_WIDTH = D_MODEL - ATTN_WIDTH
CONV_K = 3
PEER_HEADS = 8
PEER_KEYS = 128
PEER_EXPERTS = PEER_KEYS * PEER_KEYS
PEER_TOPK = 16
PEER_QDIM = 256
N_MOD = 6
EPS = 1e-6
NEG = -1e30

LANES = 128
SUBLANES = 8
VMEM_LIMIT = 56 * 1024 * 1024

Q_OFF = 0
K_OFF = ATTN_WIDTH
V_OFF = K_OFF + 2 * KV_WIDTH
CB_OFF = V_OFF + 2 * KV_WIDTH
CC_OFF = CB_OFF + CONV_WIDTH
CU_OFF = CC_OFF + CONV_WIDTH
PROJ_WIDTH = CU_OFF + CONV_WIDTH

SEQ_TILE = 512
ROUTER_TILE = 512
TOK_TILE = 512
EXP_TILE = 1024

_NN = (((1,), (0,)), ((), ()))
_NT = (((1,), (1,)), ((), ()))

f32 = jnp.float32
bf16 = jnp.bfloat16


def _split(x):
    hi = x.astype(bf16)
    lo = (x - hi.astype(f32)).astype(bf16)
    return hi, lo


def _dg(a, b, dims=_NN):
    return lax.dot_general(a, b, dims, preferred_element_type=f32)


def _dot3(a, b, dims=_NN):
    return _dg(a[0], b[0], dims) + (_dg(a[0], b[1], dims) + _dg(a[1], b[0], dims))


def _rms(x):
    return x * lax.rsqrt(jnp.mean(x * x, axis=-1, keepdims=True) + EPS)


def _mod_kernel(c_ref, w_ref, b_ref, o_ref):
    c = c_ref[...]
    ca = c * (1.0 / (1.0 + jnp.exp(-c)))
    o_ref[...] = _dot3(_split(ca), _split(w_ref[...])) + b_ref[...]


def _mod_call(c_pad, ada_w, ada_b):
    nb = 1536
    n = ada_w.shape[1]
    return pl.pallas_call(
        _mod_kernel,
        grid=(n // nb,),
        in_specs=[
            pl.BlockSpec((SUBLANES, D_MODEL), lambda j: (0, 0)),
            pl.BlockSpec((D_MODEL, nb), lambda j: (0, j)),
            pl.BlockSpec((1, nb), lambda j: (0, j)),
        ],
        out_specs=pl.BlockSpec((SUBLANES, nb), lambda j: (0, j)),
        out_shape=jax.ShapeDtypeStruct((SUBLANES, n), f32),
        compiler_params=pltpu.CompilerParams(
            dimension_semantics=("arbitrary",), vmem_limit_bytes=VMEM_LIMIT),
        name="mod",
    )(c_pad, ada_w, ada_b)


def _mixer_kernel(sink_ref, x_ref, pos_ref, mod_ref, g1_ref, winh_ref, winl_ref, gq_ref, gk_ref,
                  freq_ref, sgn_ref, msame_ref, convw_ref, ga_ref, gc_ref, wouth_ref, woutl_ref,
                  g2_ref, x1_ref, h2t_ref, qbuf, kbuf, vbuf, abuf, ucarry):
    ts = x_ref.shape[0]
    s_idx = pl.program_id(1)
    x = x_ref[...]
    mod = mod_ref[...]
    shift1, scale1, gate1 = mod[0:1], mod[1:2], mod[2:3]
    shift2, scale2 = mod[3:4], mod[4:5]

    h = _rms(x) * g1_ref[...]
    hs = _split(h * (1.0 + scale1) + shift1)

    def proj(c0, c1):
        return _dot3(hs, (winh_ref[:, c0:c1], winl_ref[:, c0:c1]))

    ang = pos_ref[...] * freq_ref[...]
    cosf = jnp.cos(ang)
    sins = jnp.sin(ang) * sgn_ref[...]
    msame = msame_ref[...]

    def headnorm_rope(t, g):
        sq = _split(t * t)
        ss = _dg(sq[0], msame) + _dg(sq[1], msame)
        tn = t * lax.rsqrt(ss * (1.0 / HEAD_DIM) + EPS) * g
        return tn * cosf + pltpu.roll(tn, LANES // 2, 1) * sins

    @pl.when(s_idx == 0)
    def _():
        kbuf[0:WINDOW, :] = jnp.zeros((WINDOW, 2 * LANES), f32)
        vbuf[0:WINDOW, :] = jnp.zeros((WINDOW, 2 * LANES), f32)
        ucarry[...] = jnp.zeros(ucarry.shape, f32)

    for c in range(ATTN_WIDTH // LANES):
        sl = slice(c * LANES, (c + 1) * LANES)
        qbuf[:, sl] = headnorm_rope(proj(Q_OFF + c * LANES, Q_OFF + (c + 1) * LANES), gq_ref[:, sl])
    for g in range(N_KV_HEADS):
        sl = slice(g * LANES, (g + 1) * LANES)
        kbuf[WINDOW:WINDOW + ts, sl] = headnorm_rope(
            proj(K_OFF + g * LANES, K_OFF + (g + 1) * LANES), gk_ref[:, sl])
    vbuf[WINDOW:WINDOW + ts, :] = proj(V_OFF, V_OFF + 2 * LANES)

    qi = lax.broadcasted_iota(jnp.int32, (WINDOW, 2 * WINDOW), 0)
    kj = lax.broadcasted_iota(jnp.int32, (WINDOW, 2 * WINDOW), 1)
    band = (kj > qi) & (kj <= qi + WINDOW)
    lane = lax.broadcasted_iota(jnp.int32, (1, LANES), 1)

    def attn_block(n, carry):
        r0 = pl.multiple_of(n * WINDOW, WINDOW)
        qb = qbuf[pl.ds(r0, WINDOW), :]
        kb = kbuf[pl.ds(r0, 2 * WINDOW), :]
        vb = vbuf[pl.ds(r0, 2 * WINDOW), :]
        first = jnp.logical_and(s_idx == 0, n == 0)
        valid = band & (kj >= jnp.where(first, WINDOW, 0))
        for c in range(ATTN_WIDTH // LANES):
            g = c // 2
            qt = qb[:, c * LANES:(c + 1) * LANES]
            kt = _split(kb[:, g * LANES:(g + 1) * LANES])
            vt = vb[:, g * LANES:(g + 1) * LANES]
            o = None
            for sub in range(2):
                qm = _split(jnp.where((lane // HALF_DIM) % 2 == sub, qt, 0.0))
                s = _dot3(qm, kt, _NT) * (HEAD_DIM ** -0.5)
                s = jnp.where(valid, s, NEG)
                sink = sink_ref[2 * c + sub]
                m = jnp.maximum(jnp.max(s, axis=-1, keepdims=True), sink)
                e = jnp.exp(s - m)
                den = jnp.sum(e, axis=-1, keepdims=True) + jnp.exp(sink - m)
                p = e / den
                vm = jnp.where(lane // HEAD_DIM == sub, vt, 0.0)
                od = _dot3(_split(p), _split(vm))
                o = od if o is None else o + od
            abuf[pl.ds(r0, WINDOW), c * LANES:(c + 1) * LANES] = o
        return carry

    lax.fori_loop(0, ts // WINDOW, attn_block, 0)
    kbuf[0:WINDOW, :] = kbuf[ts:ts + WINDOW, :]
    vbuf[0:WINDOW, :] = vbuf[ts:ts + WINDOW, :]

    cb = proj(CB_OFF, CB_OFF + CONV_WIDTH)
    u = proj(CC_OFF, CC_OFF + CONV_WIDTH) * proj(CU_OFF, CU_OFF + CONV_WIDTH)
    rowi = lax.broadcasted_iota(jnp.int32, u.shape, 0)
    prev1 = ucarry[SUBLANES - 1:SUBLANES, :]
    prev2 = ucarry[SUBLANES - 2:SUBLANES - 1, :]
    u1 = jnp.where(rowi == 0, prev1, pltpu.roll(u, 1, 0))
    u2 = jnp.where(rowi == 0, prev2, jnp.where(rowi == 1, prev1, pltpu.roll(u, 2, 0)))
    ucarry[...] = u[ts - SUBLANES:ts, :]
    w = convw_ref[...]
    conv = cb * (w[0:1] * u2 + w[1:2] * u1 + w[2:3] * u)

    ra = _split(_rms(abuf[...]) * ga_ref[...])
    rc = _split(_rms(conv) * gc_ref[...])
    y = (_dot3(ra, (wouth_ref[0:ATTN_WIDTH, :], woutl_ref[0:ATTN_WIDTH, :]))
         + _dot3(rc, (wouth_ref[ATTN_WIDTH:, :], woutl_ref[ATTN_WIDTH:, :])))
    x1 = x + gate1 * y
    x1_ref[...] = x1
    h2 = _rms(x1) * g2_ref[...]
    h2t_ref[...] = (h2 * (1.0 + scale2) + shift2).T


def _const_spec(shape):
    nd = len(shape)
    return pl.BlockSpec(shape, lambda *_: (0,) * nd, pipeline_mode=pl.Buffered(1))


def _mixer_call(sinks, x, pos, mod, g1, winh, winl, gq, gk, freq, sgn, msame, convw, ga, gc,
                wouth, woutl, g2):
    B, S, D = x.shape
    ts = SEQ_TILE
    nst = S // ts
    return pl.pallas_call(
        _mixer_kernel,
        grid=(B, nst),
        in_specs=[
            pl.BlockSpec(memory_space=pltpu.SMEM),
            pl.BlockSpec((None, ts, D), lambda b, s: (b, s, 0)),
            pl.BlockSpec((None, ts, 1), lambda b, s: (b, s, 0)),
            pl.BlockSpec((None, N_MOD, D), lambda b, s: (b, 0, 0)),
            _const_spec(g1.shape), _const_spec(winh.shape), _const_spec(winl.shape),
            _const_spec(gq.shape), _const_spec(gk.shape), _const_spec(freq.shape),
            _const_spec(sgn.shape), _const_spec(msame.shape), _const_spec(convw.shape),
            _const_spec(ga.shape), _const_spec(gc.shape), _const_spec(wouth.shape),
            _const_spec(woutl.shape), _const_spec(g2.shape),
        ],
        out_specs=[
            pl.BlockSpec((None, ts, D), lambda b, s: (b, s, 0)),
            pl.BlockSpec((D, ts), lambda b, s: (0, b * nst + s)),
        ],
        out_shape=[
            jax.ShapeDtypeStruct((B, S, D), f32),
            jax.ShapeDtypeStruct((D, B * S), f32),
        ],
        scratch_shapes=[
            pltpu.VMEM((ts, ATTN_WIDTH), f32),
            pltpu.VMEM((WINDOW + ts, 2 * LANES), f32),
            pltpu.VMEM((WINDOW + ts, 2 * LANES), f32),
            pltpu.VMEM((ts, ATTN_WIDTH), f32),
            pltpu.VMEM((SUBLANES, CONV_WIDTH), f32),
        ],
        compiler_params=pltpu.CompilerParams(
            dimension_semantics=("arbitrary", "arbitrary"), vmem_limit_bytes=VMEM_LIMIT),
        name="mixer",
    )(sinks, x, pos, mod, g1, winh, winl, gq, gk, freq, sgn, msame, convw, ga, gc, wouth, woutl, g2)


def _oddeven_merge_sort_pairs(n):
    pairs = []
    p = 1
    while p < n:
        k = p
        while k >= 1:
            for j in range(k % p, n - k, 2 * k):
                for i in range(min(k, n - j - k)):
                    if (i + j) // (2 * p) == (i + j + k) // (2 * p):
                        pairs.append((i + j, i + j + k))
            k //= 2
        p *= 2
    return pairs


_SORT16 = _oddeven_merge_sort_pairs(PEER_TOPK)


def _bitonic_desc(z):
    z = list(z)
    d = PEER_TOPK // 2
    while d >= 1:
        for r in range(PEER_TOPK):
            if not r & d:
                hi, lo = jnp.maximum(z[r], z[r + d]), jnp.minimum(z[r], z[r + d])
                z[r], z[r + d] = hi, lo
        d //= 2
    return z


def _merge_top(R, L):
    z = list(R)
    for r in range(PEER_TOPK - len(L), PEER_TOPK):
        z[r] = jnp.maximum(R[r], L[PEER_TOPK - 1 - r])
    return _bitonic_desc(z)


def _top16_rows(sc):
    x = [sc[SUBLANES * g:SUBLANES * (g + 1), :] for g in range(PEER_KEYS // SUBLANES)]
    for i, j in _SORT16:
        x[i], x[j] = jnp.maximum(x[i], x[j]), jnp.minimum(x[i], x[j])
    for shift in (4, 2, 1):
        z = [jnp.maximum(x[r], pltpu.roll(x[PEER_TOPK - 1 - r], shift, 0)) for r in range(PEER_TOPK)]
        x = _bitonic_desc(z)
    return x


_STAIR = [(r1, r2) for r1 in range(PEER_TOPK) for r2 in range(PEER_TOPK)
          if (r1 + 1) * (r2 + 1) <= PEER_TOPK]


def _router_kernel(h2t_ref, wqh_ref, wql_ref, skh_ref, skl_ref,
                   f1_ref, f2_ref, kap_ref, h2b_ref, q_s, sc_s, top_s, row_s):
    tr = h2t_ref.shape[1]
    hs = _split(h2t_ref[...])
    h2b_ref[...] = hs[0]
    q_s[...] = _dot3((wqh_ref[...], wql_ref[...]), hs)
    top_s[...] = jnp.zeros(top_s.shape, f32)
    sub = lax.broadcasted_iota(jnp.int32, (SUBLANES, tr), 0)

    def head_body(h, carry):
        for p in range(2):
            hp = 2 * h + p
            r0 = pl.multiple_of(hp * PEER_KEYS, PEER_KEYS)
            sc = _dot3((skh_ref[hp], skl_ref[hp]), _split(q_s[pl.ds(r0, PEER_KEYS), :]))
            sc_s[hp] = sc
            top = _top16_rows(sc)
            for r in range(PEER_TOPK):
                top_s[p, r] = jnp.where(sub == h, top[r], top_s[p, r])
        return carry

    lax.fori_loop(0, PEER_HEADS, head_body, 0)

    a = [top_s[0, r] for r in range(PEER_TOPK)]
    b = [top_s[1, r] for r in range(PEER_TOPK)]
    R = [a[0] + b[r] for r in range(16)]
    R = _merge_top(R, [a[r] + b[0] for r in range(1, 16)])
    R = _merge_top(R, [a[1] + b[r] for r in range(1, 8)])
    R = _merge_top(R, [a[r] + b[1] for r in range(2, 8)])
    R = _merge_top(R, [a[2] + b[r] for r in range(2, 5)])
    R = _merge_top(R, [a[r] + b[2] for r in range(3, 5)])
    R = _merge_top(R, [a[3] + b[3]])
    tau = R[PEER_TOPK - 1]
    z = jnp.exp(R[0] - R[0])
    for r in range(1, PEER_TOPK):
        z = z + jnp.exp(R[r] - R[0])
    inv_z = 1.0 / z
    fa = [jnp.exp(a[r] - a[0]) * inv_z for r in range(PEER_TOPK)]
    fb = [jnp.exp(b[r] - b[0]) for r in range(PEER_TOPK)]
    kap = jnp.full((SUBLANES, tr), jnp.inf, f32)
    for r1, r2 in _STAIR:
        kap = jnp.minimum(kap, jnp.where(a[r1] + b[r2] >= tau, fa[r1] * fb[r2], jnp.inf))
    kap_ref[...] = kap
    row_s[0] = a[0]
    row_s[1] = b[0]
    row_s[2] = inv_z

    def fac_body(h, carry):
        a0 = row_s[0, pl.ds(h, 1), :]
        b0 = row_s[1, pl.ds(h, 1), :]
        iz = row_s[2, pl.ds(h, 1), :]
        f1_ref[h] = jnp.exp(sc_s[2 * h] - a0) * iz
        f2_ref[h] = jnp.exp(sc_s[2 * h + 1] - b0)
        return carry

    lax.fori_loop(0, PEER_HEADS, fac_body, 0)


def _router_call(h2t, wqh, wql, skh, skl):
    D, T = h2t.shape
    tr = ROUTER_TILE
    return pl.pallas_call(
        _router_kernel,
        grid=(T // tr,),
        in_specs=[
            pl.BlockSpec((D, tr), lambda i: (0, i)),
            _const_spec(wqh.shape), _const_spec(wql.shape),
            _const_spec(skh.shape), _const_spec(skl.shape),
        ],
        out_specs=[
            pl.BlockSpec((PEER_HEADS, PEER_KEYS, tr), lambda i: (0, 0, i)),
            pl.BlockSpec((PEER_HEADS, PEER_KEYS, tr), lambda i: (0, 0, i)),
            pl.BlockSpec((PEER_HEADS, tr), lambda i: (0, i)),
            pl.BlockSpec((D, tr), lambda i: (0, i)),
        ],
        out_shape=[
            jax.ShapeDtypeStruct((PEER_HEADS, PEER_KEYS, T), f32),
            jax.ShapeDtypeStruct((PEER_HEADS, PEER_KEYS, T), f32),
            jax.ShapeDtypeStruct((PEER_HEADS, T), f32),
            jax.ShapeDtypeStruct((D, T), bf16),
        ],
        scratch_shapes=[
            pltpu.VMEM((PEER_HEADS * PEER_QDIM, tr), f32),
            pltpu.VMEM((2 * PEER_HEADS, PEER_KEYS, tr), f32),
            pltpu.VMEM((2, PEER_TOPK, SUBLANES, tr), f32),
            pltpu.VMEM((3, SUBLANES, tr), f32),
        ],
        compiler_params=pltpu.CompilerParams(
            dimension_semantics=("arbitrary",), vmem_limit_bytes=VMEM_LIMIT),
        name="router",
    )(h2t, wqh, wql, skh, skl)


_GELU_C1 = float(np.sqrt(2.0 / np.pi))
_GELU_C2 = 0.044715 * _GELU_C1


N_CHUNKS = PEER_EXPERTS // EXP_TILE


def _experts_kernel(h2b_ref, f1_ref, f2_ref, kap_ref, u_ref, vt_ref, x1_ref, mod_ref,
                    out_ref, acc, a0_s, a1_s, w0_s, w1_s):
    tm = h2b_ref.shape[1]
    te = u_ref.shape[0]
    assert te == SUBLANES * PEER_KEYS
    s = pl.program_id(0)
    c1 = jnp.maximum(s - 1, 0) % N_CHUNKS
    c2 = jnp.maximum(s - 2, 0) % N_CHUNKS

    @pl.when(s == 0)
    def _():
        for ref in (a0_s, a1_s, w0_s, w1_s):
            ref[...] = jnp.zeros(ref.shape, ref.dtype)

    @pl.when(c2 == 0)
    def _():
        acc[...] = jnp.zeros(acc.shape, f32)

    def step(a_cur, a_prev, w_prev, w_cur):
        i0 = pl.multiple_of(c1 * SUBLANES, SUBLANES)
        mxu_n = 2 * LANES
        for r in range(SUBLANES):
            rows = slice(r * PEER_KEYS, (r + 1) * PEER_KEYS)
            drows = slice(r * (D_MODEL // SUBLANES), (r + 1) * (D_MODEL // SUBLANES))
            for tc in range(tm // LANES):
                cols = slice(tc * LANES, (tc + 1) * LANES)
                wide = slice((tc // 2) * mxu_n, (tc // 2 + 1) * mxu_n)
                if tc % 2 == 0:
                    a_cur[rows, wide] = _dg(u_ref[rows, :], h2b_ref[:, wide])
                g = jnp.zeros((PEER_KEYS, LANES), f32)
                for h in range(PEER_HEADS):
                    f1 = f1_ref[h, pl.ds(i0, SUBLANES), cols][r:r + 1, :]
                    p = f1 * f2_ref[h, :, cols]
                    g = g + jnp.where(p >= kap_ref[h:h + 1, cols], p, 0.0)
                a = a_prev[rows, cols]
                t = jnp.tanh(a * (_GELU_C1 + _GELU_C2 * (a * a)))
                w_prev[rows, cols] = ((0.5 * g) * (a * (1.0 + t))).astype(bf16)
                if tc % 2 == 1:
                    acc[drows, wide] += _dg(vt_ref[drows, :], w_cur[:, wide])

    @pl.when(s % 2 == 0)
    def _():
        step(a0_s, a1_s, w1_s, w0_s)

    @pl.when(s % 2 == 1)
    def _():
        step(a1_s, a0_s, w0_s, w1_s)

    @pl.when(jnp.logical_and(s >= 2, c2 == N_CHUNKS - 1))
    def _():
        gate2 = mod_ref[N_MOD - 1:N_MOD, :]
        out_ref[...] = x1_ref[...] + gate2 * acc[...].T


def _experts_call(h2b, f1, f2, kap, u_b, vt_b, x1, mod, tiles_per_batch):
    D, T = h2b.shape
    tm, te = TOK_TILE, EXP_TILE
    n_pairs = (T // tm) * N_CHUNKS

    def pair(s, lag):
        return jnp.clip(s - lag, 0, n_pairs - 1)

    def tile(s, lag):
        return pair(s, lag) // N_CHUNKS

    def chunk(s, lag):
        return pair(s, lag) % N_CHUNKS

    return pl.pallas_call(
        _experts_kernel,
        grid=(n_pairs + 2,),
        in_specs=[
            pl.BlockSpec((D, tm), lambda s: (0, tile(s, 0))),
            pl.BlockSpec((PEER_HEADS, PEER_KEYS, tm), lambda s: (0, 0, tile(s, 1))),
            pl.BlockSpec((PEER_HEADS, PEER_KEYS, tm), lambda s: (0, 0, tile(s, 1))),
            pl.BlockSpec((PEER_HEADS, tm), lambda s: (0, tile(s, 1))),
            pl.BlockSpec((te, D), lambda s: (chunk(s, 0), 0)),
            pl.BlockSpec((D, te), lambda s: (0, chunk(s, 2))),
            pl.BlockSpec((tm, D), lambda s: (tile(s, 2), 0)),
            pl.BlockSpec((None, N_MOD, D), lambda s: (tile(s, 2) // tiles_per_batch, 0, 0)),
        ],
        out_specs=pl.BlockSpec((tm, D), lambda s: (tile(s, 2), 0)),
        out_shape=jax.ShapeDtypeStruct((T, D), f32),
        scratch_shapes=[
            pltpu.VMEM((D, tm), f32),
            pltpu.VMEM((te, tm), f32),
            pltpu.VMEM((te, tm), f32),
            pltpu.VMEM((te, tm), bf16),
            pltpu.VMEM((te, tm), bf16),
        ],
        compiler_params=pltpu.CompilerParams(
            dimension_semantics=("arbitrary",), vmem_limit_bytes=VMEM_LIMIT),
        name="experts",
    )(h2b, f1, f2, kap, u_b, vt_b, x1, mod)


def _proj_columns():
    half = np.arange(HALF_DIM)
    cols = []
    for c in range(N_Q_HEADS // 2):
        for part in range(2):
            for head in (2 * c, 2 * c + 1):
                cols.append(head * HEAD_DIM + part * HALF_DIM + half)
    for g in range(N_KV_HEADS):
        for part in (0, 0, 1, 1):
            cols.append(ATTN_WIDTH + g * HEAD_DIM + part * HALF_DIM + half)
    for g in range(N_KV_HEADS):
        for _ in range(2):
            cols.append(ATTN_WIDTH + KV_WIDTH + g * HEAD_DIM + np.arange(HEAD_DIM))
    cols.append(np.arange(ATTN_WIDTH + 2 * KV_WIDTH, ATTN_WIDTH + 2 * KV_WIDTH + 3 * CONV_WIDTH))
    return np.concatenate(cols)


_PROJ_COLS = _proj_columns()
_TILE_DIM = np.concatenate([np.arange(HALF_DIM), np.arange(HALF_DIM),
                            HALF_DIM + np.arange(HALF_DIM), HALF_DIM + np.arange(HALF_DIM)])
_TILE_HEAD = (np.arange(LANES) // HALF_DIM) % 2
_MSAME = (_TILE_HEAD[:, None] == _TILE_HEAD[None, :]).astype(np.float32)
_INV_FREQ = (ROPE_THETA ** (-np.arange(0, HEAD_DIM, 2, dtype=np.float32) / HEAD_DIM)).astype(np.float32)


def kernel(x, c, positions, ada_w, ada_b, norm1_g, w_in, q_norm_g, k_norm_g, sinks, conv_w,
           attn_out_g, conv_out_g, w_out, norm2_g, peer_wq, peer_subkeys, peer_u, peer_v):
    B, S, D = x.shape
    T = B * S
    assert ada_w.shape[0] == 1, "single-layer block"
    assert D == D_MODEL and S % SEQ_TILE == 0 and T % TOK_TILE == 0 and T % ROUTER_TILE == 0
    assert S % TOK_TILE == 0

    c_pad = jnp.pad(c, ((0, SUBLANES - B), (0, 0)))
    mod = _mod_call(c_pad, ada_w[0], ada_b[0][None, :])[:B].reshape(B, N_MOD, D)

    inv_freq = ROPE_THETA ** (-jnp.arange(0, HEAD_DIM, 2, dtype=f32) / HEAD_DIM)
    freq = inv_freq[_TILE_DIM % HALF_DIM][None, :]
    sgn = jnp.asarray(np.where(np.arange(LANES) < LANES // 2, -1.0, 1.0), f32)[None, :]
    winh, winl = _split(w_in[0][:, _PROJ_COLS])
    wouth, woutl = _split(w_out[0])
    gq = jnp.tile(q_norm_g[0][_TILE_DIM], ATTN_WIDTH // LANES)[None, :]
    gk = jnp.tile(k_norm_g[0][_TILE_DIM], N_KV_HEADS)[None, :]
    x1, h2t = _mixer_call(
        sinks[0], x, positions.astype(f32)[..., None], mod, norm1_g, winh, winl, gq, gk, freq, sgn,
        jnp.asarray(_MSAME, bf16), conv_w[0], attn_out_g, conv_out_g, wouth, woutl, norm2_g)

    wqh, wql = _split(peer_wq[0].T)
    skh, skl = _split(peer_subkeys[0].reshape(2 * PEER_HEADS, PEER_KEYS, PEER_QDIM // 2))
    f1, f2, kap, h2b = _router_call(h2t, wqh, wql, skh, skl)

    out = _experts_call(h2b, f1, f2, kap, peer_u[0].astype(bf16), peer_v[0].T.astype(bf16),
                        x1.reshape(T, D), mod, S // TOK_TILE)
    return out.reshape(B, S, D)
```

```python
import numpy as np
import jax
import jax.numpy as jnp
from jax import lax
from jax.experimental import pallas as pl
from jax.experimental.pallas import tpu as pltpu

D_MODEL = 1024
HEAD_DIM = 64
HALF_DIM = HEAD_DIM // 2
N_Q_HEADS = 8
N_KV_HEADS = 2
ATTN_WIDTH = N_Q_HEADS * HEAD_DIM
KV_WIDTH = N_KV_HEADS * HEAD_DIM
WINDOW = 128
ROPE_THETA = 10000.0
CONV_WIDTH = D_MODEL - ATTN_WIDTH
CONV_K = 3
PEER_HEADS = 8
PEER_KEYS = 128
PEER_EXPERTS = PEER_KEYS * PEER_KEYS
PEER_TOPK = 16
PEER_QDIM = 256
N_MOD = 6
EPS = 1e-6
NEG = -1e30

LANES = 128
SUBLANES = 8
BF16_ROWS = 2 * SUBLANES
VMEM_LIMIT = 56 * 1024 * 1024

Q_OFF = 0
K_OFF = ATTN_WIDTH
V_OFF = K_OFF + 2 * KV_WIDTH
CB_OFF = V_OFF + 2 * KV_WIDTH
CC_OFF = CB_OFF + CONV_WIDTH
CU_OFF = CC_OFF + CONV_WIDTH
PROJ_WIDTH = CU_OFF + CONV_WIDTH

SEQ_TILE = 512
ROUTER_TILE = 512
TOK_TILE = 512
EXP_TILE = 1024

_NN = (((1,), (0,)), ((), ()))
_NT = (((1,), (1,)), ((), ()))

f32 = jnp.float32
bf16 = jnp.bfloat16


def _split(x):
    hi = x.astype(bf16)
    lo = (x - hi.astype(f32)).astype(bf16)
    return hi, lo


def _dg(a, b, dims=_NN):
    return lax.dot_general(a, b, dims, preferred_element_type=f32)


def _dot3(a, b, dims=_NN):
    return _dg(a[0], b[0], dims) + (_dg(a[0], b[1], dims) + _dg(a[1], b[0], dims))


def _rms(x):
    return x * lax.rsqrt(jnp.mean(x * x, axis=-1, keepdims=True) + EPS)


def _mod_kernel(c_ref, w_ref, b_ref, o_ref):
    c = c_ref[...]
    ca = c * (1.0 / (1.0 + jnp.exp(-c)))
    o_ref[...] = _dot3(_split(ca), _split(w_ref[...])) + b_ref[...]


def _mod_call(c_pad, ada_w, ada_b):
    nb = 1536
    n = ada_w.shape[1]
    return pl.pallas_call(
        _mod_kernel,
        grid=(n // nb,),
        in_specs=[
            pl.BlockSpec((SUBLANES, D_MODEL), lambda j: (0, 0)),
            pl.BlockSpec((D_MODEL, nb), lambda j: (0, j)),
            pl.BlockSpec((1, nb), lambda j: (0, j)),
        ],
        out_specs=pl.BlockSpec((SUBLANES, nb), lambda j: (0, j)),
        out_shape=jax.ShapeDtypeStruct((SUBLANES, n), f32),
        compiler_params=pltpu.CompilerParams(
            dimension_semantics=("arbitrary",), vmem_limit_bytes=VMEM_LIMIT),
        name="mod",
    )(c_pad, ada_w, ada_b)


def _mixer_kernel(sink_ref, x_ref, pos_ref, mod_ref, g1_ref, win_ref, gq_ref, gk_ref,
                  freq_ref, sgn_ref, msame_ref, convw_ref, ga_ref, gc_ref, wout_ref,
                  g2_ref, x1_ref, h2t_ref, qbuf, kbuf, vbuf, abuf, ucarry):
    ts = x_ref.shape[0]
    s_idx = pl.program_id(1)
    x = x_ref[...]
    mod = mod_ref[...]
    shift1, scale1, gate1 = mod[0:1], mod[1:2], mod[2:3]
    shift2, scale2 = mod[3:4], mod[4:5]

    h = _rms(x) * g1_ref[...]
    hb = (h * (1.0 + scale1) + shift1).astype(bf16)

    def proj(c0, c1):
        return _dg(hb, win_ref[:, c0:c1])

    ang = pos_ref[...] * freq_ref[...]
    cosf = jnp.cos(ang)
    sins = jnp.sin(ang) * sgn_ref[...]
    msame = msame_ref[...]

    def headnorm_rope(t, g):
        sq = _split(t * t)
        ss = _dg(sq[0], msame) + _dg(sq[1], msame)
        tn = t * lax.rsqrt(ss * (1.0 / HEAD_DIM) + EPS) * g
        return tn * cosf + pltpu.roll(tn, LANES // 2, 1) * sins

    @pl.when(s_idx == 0)
    def _():
        kbuf[0:WINDOW, :] = jnp.zeros((WINDOW, 2 * LANES), f32)
        vbuf[0:WINDOW, :] = jnp.zeros((WINDOW, 2 * LANES), f32)
        ucarry[...] = jnp.zeros(ucarry.shape, f32)

    for c in range(ATTN_WIDTH // LANES):
        sl = slice(c * LANES, (c + 1) * LANES)
        qbuf[:, sl] = headnorm_rope(proj(Q_OFF + c * LANES, Q_OFF + (c + 1) * LANES), gq_ref[:, sl])
    for g in range(N_KV_HEADS):
        sl = slice(g * LANES, (g + 1) * LANES)
        kbuf[WINDOW:WINDOW + ts, sl] = headnorm_rope(
            proj(K_OFF + g * LANES, K_OFF + (g + 1) * LANES), gk_ref[:, sl])
    vbuf[WINDOW:WINDOW + ts, :] = proj(V_OFF, V_OFF + 2 * LANES)

    qi = lax.broadcasted_iota(jnp.int32, (WINDOW, 2 * WINDOW), 0)
    kj = lax.broadcasted_iota(jnp.int32, (WINDOW, 2 * WINDOW), 1)
    band = (kj > qi) & (kj <= qi + WINDOW)
    lane = lax.broadcasted_iota(jnp.int32, (1, LANES), 1)

    def attn_block(n, carry):
        r0 = pl.multiple_of(n * WINDOW, WINDOW)
        qb = qbuf[pl.ds(r0, WINDOW), :]
        kb = kbuf[pl.ds(r0, 2 * WINDOW), :]
        vb = vbuf[pl.ds(r0, 2 * WINDOW), :]
        first = jnp.logical_and(s_idx == 0, n == 0)
        valid = band & (kj >= jnp.where(first, WINDOW, 0))
        for c in range(ATTN_WIDTH // LANES):
            g = c // 2
            qt = qb[:, c * LANES:(c + 1) * LANES]
            kt = kb[:, g * LANES:(g + 1) * LANES].astype(bf16)
            vt = vb[:, g * LANES:(g + 1) * LANES]
            o = None
            for sub in range(2):
                qm = jnp.where((lane // HALF_DIM) % 2 == sub, qt, 0.0).astype(bf16)
                s = _dg(qm, kt, _NT) * (HEAD_DIM ** -0.5)
                s = jnp.where(valid, s, NEG)
                sink = sink_ref[2 * c + sub]
                m = jnp.maximum(jnp.max(s, axis=-1, keepdims=True), sink)
                e = jnp.exp(s - m)
                den = jnp.sum(e, axis=-1, keepdims=True) + jnp.exp(sink - m)
                p = e / den
                vm = jnp.where(lane // HEAD_DIM == sub, vt, 0.0).astype(bf16)
                od = _dg(p.astype(bf16), vm)
                o = od if o is None else o + od
            abuf[pl.ds(r0, WINDOW), c * LANES:(c + 1) * LANES] = o
        return carry

    lax.fori_loop(0, ts // WINDOW, attn_block, 0)
    kbuf[0:WINDOW, :] = kbuf[ts:ts + WINDOW, :]
    vbuf[0:WINDOW, :] = vbuf[ts:ts + WINDOW, :]

    cb = proj(CB_OFF, CB_OFF + CONV_WIDTH)
    u = proj(CC_OFF, CC_OFF + CONV_WIDTH) * proj(CU_OFF, CU_OFF + CONV_WIDTH)
    rowi = lax.broadcasted_iota(jnp.int32, u.shape, 0)
    prev1 = ucarry[SUBLANES - 1:SUBLANES, :]
    prev2 = ucarry[SUBLANES - 2:SUBLANES - 1, :]
    u1 = jnp.where(rowi == 0, prev1, pltpu.roll(u, 1, 0))
    u2 = jnp.where(rowi == 0, prev2, jnp.where(rowi == 1, prev1, pltpu.roll(u, 2, 0)))
    ucarry[...] = u[ts - SUBLANES:ts, :]
    w = convw_ref[...]
    conv = cb * (w[0:1] * u2 + w[1:2] * u1 + w[2:3] * u)

    ra = (_rms(abuf[...]) * ga_ref[...]).astype(bf16)
    rc = (_rms(conv) * gc_ref[...]).astype(bf16)
    y = _dg(ra, wout_ref[0:ATTN_WIDTH, :]) + _dg(rc, wout_ref[ATTN_WIDTH:, :])
    x1 = x + gate1 * y
    x1_ref[...] = x1
    h2 = _rms(x1) * g2_ref[...]
    h2t_ref[...] = (h2 * (1.0 + scale2) + shift2).T


def _const_spec(shape):
    nd = len(shape)
    return pl.BlockSpec(shape, lambda *_: (0,) * nd, pipeline_mode=pl.Buffered(1))


def _mixer_call(sinks, x, pos, mod, g1, win, gq, gk, freq, sgn, msame, convw, ga, gc, wout, g2):
    B, S, D = x.shape
    ts = SEQ_TILE
    nst = S // ts
    return pl.pallas_call(
        _mixer_kernel,
        grid=(B, nst),
        in_specs=[
            pl.BlockSpec(memory_space=pltpu.SMEM),
            pl.BlockSpec((None, ts, D), lambda b, s: (b, s, 0)),
            pl.BlockSpec((None, ts, 1), lambda b, s: (b, s, 0)),
            pl.BlockSpec((None, N_MOD, D), lambda b, s: (b, 0, 0)),
            _const_spec(g1.shape), _const_spec(win.shape),
            _const_spec(gq.shape), _const_spec(gk.shape), _const_spec(freq.shape),
            _const_spec(sgn.shape), _const_spec(msame.shape), _const_spec(convw.shape),
            _const_spec(ga.shape), _const_spec(gc.shape), _const_spec(wout.shape),
            _const_spec(g2.shape),
        ],
        out_specs=[
            pl.BlockSpec((None, ts, D), lambda b, s: (b, s, 0)),
            pl.BlockSpec((D, ts), lambda b, s: (0, b * nst + s)),
        ],
        out_shape=[
            jax.ShapeDtypeStruct((B, S, D), f32),
            jax.ShapeDtypeStruct((D, B * S), f32),
        ],
        scratch_shapes=[
            pltpu.VMEM((ts, ATTN_WIDTH), f32),
            pltpu.VMEM((WINDOW + ts, 2 * LANES), f32),
            pltpu.VMEM((WINDOW + ts, 2 * LANES), f32),
            pltpu.VMEM((ts, ATTN_WIDTH), f32),
            pltpu.VMEM((SUBLANES, CONV_WIDTH), f32),
        ],
        compiler_params=pltpu.CompilerParams(
            dimension_semantics=("arbitrary", "arbitrary"), vmem_limit_bytes=VMEM_LIMIT),
        name="mixer",
    )(sinks, x, pos, mod, g1, win, gq, gk, freq, sgn, msame, convw, ga, gc, wout, g2)


def _oddeven_merge_sort_pairs(n):
    pairs = []
    p = 1
    while p < n:
        k = p
        while k >= 1:
            for j in range(k % p, n - k, 2 * k):
                for i in range(min(k, n - j - k)):
                    if (i + j) // (2 * p) == (i + j + k) // (2 * p):
                        pairs.append((i + j, i + j + k))
            k //= 2
        p *= 2
    return pairs


_SORT16 = _oddeven_merge_sort_pairs(PEER_TOPK)


def _bitonic_desc(z):
    z = list(z)
    d = PEER_TOPK // 2
    while d >= 1:
        for r in range(PEER_TOPK):
            if not r & d:
                hi, lo = jnp.maximum(z[r], z[r + d]), jnp.minimum(z[r], z[r + d])
                z[r], z[r + d] = hi, lo
        d //= 2
    return z


def _merge_top(R, L):
    z = list(R)
    for r in range(PEER_TOPK - len(L), PEER_TOPK):
        z[r] = jnp.maximum(R[r], L[PEER_TOPK - 1 - r])
    return _bitonic_desc(z)


def _top16_rows(sc):
    x = [sc[SUBLANES * g:SUBLANES * (g + 1), :] for g in range(PEER_KEYS // SUBLANES)]
    for i, j in _SORT16:
        x[i], x[j] = jnp.maximum(x[i], x[j]), jnp.minimum(x[i], x[j])
    for shift in (4, 2, 1):
        z = [jnp.maximum(x[r], pltpu.roll(x[PEER_TOPK - 1 - r], shift, 0)) for r in range(PEER_TOPK)]
        x = _bitonic_desc(z)
    return x


def _router_kernel(h2t_ref, wqh_ref, wql_ref, skh_ref, skl_ref,
                   f1_ref, cnt_ref, rk_ref, f2_ref, h2b_ref, q_s, sc_s, top_s, row_s):
    tr = h2t_ref.shape[1]
    hs = _split(h2t_ref[...])
    h2b_ref[...] = hs[0]
    q_s[...] = _dot3((wqh_ref[...], wql_ref[...]), hs)
    top_s[...] = jnp.zeros(top_s.shape, f32)
    sub = lax.broadcasted_iota(jnp.int32, (SUBLANES, tr), 0)

    def head_body(h, carry):
        for p in range(2):
            hp = 2 * h + p
            r0 = pl.multiple_of(hp * PEER_KEYS, PEER_KEYS)
            sc = _dot3((skh_ref[hp], skl_ref[hp]), _split(q_s[pl.ds(r0, PEER_KEYS), :]))
            sc_s[hp] = sc
            top = _top16_rows(sc)
            for r in range(PEER_TOPK):
                top_s[p, r] = jnp.where(sub == h, top[r], top_s[p, r])
        return carry

    lax.fori_loop(0, PEER_HEADS, head_body, 0)

    a = [top_s[0, r] for r in range(PEER_TOPK)]
    b = [top_s[1, r] for r in range(PEER_TOPK)]
    R = [a[0] + b[r] for r in range(16)]
    R = _merge_top(R, [a[r] + b[0] for r in range(1, 16)])
    R = _merge_top(R, [a[1] + b[r] for r in range(1, 8)])
    R = _merge_top(R, [a[r] + b[1] for r in range(2, 8)])
    R = _merge_top(R, [a[2] + b[r] for r in range(2, 5)])
    R = _merge_top(R, [a[r] + b[2] for r in range(3, 5)])
    R = _merge_top(R, [a[3] + b[3]])
    tau = R[PEER_TOPK - 1]
    z = jnp.exp(R[0] - R[0])
    for r in range(1, PEER_TOPK):
        z = z + jnp.exp(R[r] - R[0])
    row_s[0] = a[0]
    row_s[1] = b[0]
    row_s[2] = 0.5 / z
    row_s[3] = tau

    def fac_body(h, carry):
        row = pl.ds(h, 1)
        s1 = sc_s[2 * h]
        s2 = sc_s[2 * h + 1]
        tau_h = row_s[3, row, :]
        cnt = jnp.zeros(s1.shape, f32)
        rk = jnp.zeros(s2.shape, f32)
        for r in range(PEER_TOPK):
            b_r = top_s[1, r, row, :]
            cnt = jnp.where(s1 + b_r >= tau_h, float(r + 1), cnt)
            rk = jnp.where(b_r > s2, float(r + 1), rk)
        f1_ref[h] = jnp.exp(s1 - row_s[0, row, :]) * row_s[2, row, :]
        cnt_ref[h] = cnt
        rk_ref[h] = rk
        f2_ref[h] = jnp.exp(s2 - row_s[1, row, :])
        return carry

    lax.fori_loop(0, PEER_HEADS, fac_body, 0)


def _router_call(h2t, wqh, wql, skh, skl):
    D, T = h2t.shape
    tr = ROUTER_TILE
    per_tok = pl.BlockSpec((PEER_HEADS, PEER_KEYS, tr), lambda i: (0, 0, i))
    return pl.pallas_call(
        _router_kernel,
        grid=(T // tr,),
        in_specs=[
            pl.BlockSpec((D, tr), lambda i: (0, i)),
            _const_spec(wqh.shape), _const_spec(wql.shape),
            _const_spec(skh.shape), _const_spec(skl.shape),
        ],
        out_specs=[per_tok, per_tok, per_tok, per_tok, pl.BlockSpec((D, tr), lambda i: (0, i))],
        out_shape=[
            jax.ShapeDtypeStruct((PEER_HEADS, PEER_KEYS, T), f32),
            jax.ShapeDtypeStruct((PEER_HEADS, PEER_KEYS, T), f32),
            jax.ShapeDtypeStruct((PEER_HEADS, PEER_KEYS, T), f32),
            jax.ShapeDtypeStruct((PEER_HEADS, PEER_KEYS, T), f32),
            jax.ShapeDtypeStruct((D, T), bf16),
        ],
        scratch_shapes=[
            pltpu.VMEM((PEER_HEADS * PEER_QDIM, tr), f32),
            pltpu.VMEM((2 * PEER_HEADS, PEER_KEYS, tr), f32),
            pltpu.VMEM((2, PEER_TOPK, SUBLANES, tr), f32),
            pltpu.VMEM((4, SUBLANES, tr), f32),
        ],
        compiler_params=pltpu.CompilerParams(
            dimension_semantics=("arbitrary",), vmem_limit_bytes=VMEM_LIMIT),
        name="router",
    )(h2t, wqh, wql, skh, skl)


_GELU_C1 = float(np.sqrt(2.0 / np.pi))
_GELU_C2 = 0.044715 * _GELU_C1


def _experts_kernel(h2b_ref, f1_ref, cnt_ref, rk_ref, f2_ref, u_ref, vt_ref, x1_ref, mod_ref,
                    out_ref, acc, a_s, w_s, rk_s, f2_s):
    tm = h2b_ref.shape[1]
    te = u_ref.shape[0]
    assert te == SUBLANES * PEER_KEYS
    j = pl.program_id(1)

    @pl.when(j == 0)
    def _():
        acc[...] = jnp.zeros(acc.shape, f32)
        for h in range(PEER_HEADS):
            rk_s[h] = rk_ref[h].astype(bf16)
            f2_s[h] = f2_ref[h].astype(bf16)

    a_s[...] = _dg(u_ref[...], h2b_ref[...])

    i0 = pl.multiple_of(j * SUBLANES, SUBLANES)

    def packed_row(ref, h, r, cols):
        row = ref[h, pl.ds(i0, SUBLANES), cols][r:r + 1, :]
        return jnp.broadcast_to(row, (BF16_ROWS, LANES)).astype(bf16)

    for r in range(SUBLANES):
        for tc in range(tm // LANES):
            cols = slice(tc * LANES, (tc + 1) * LANES)
            f1 = [packed_row(f1_ref, h, r, cols) for h in range(PEER_HEADS)]
            cnt = [packed_row(cnt_ref, h, r, cols) for h in range(PEER_HEADS)]
            for v in range(PEER_KEYS // BF16_ROWS):
                keys = slice(v * BF16_ROWS, (v + 1) * BF16_ROWS)
                rows = slice(r * PEER_KEYS + v * BF16_ROWS, r * PEER_KEYS + (v + 1) * BF16_ROWS)
                g = jnp.zeros((BF16_ROWS, LANES), bf16)
                for h in range(PEER_HEADS):
                    sel = jnp.where(rk_s[h, keys, cols] < cnt[h], f2_s[h, keys, cols],
                                    jnp.zeros((), bf16))
                    g = g + sel * f1[h]
                a = a_s[rows, cols]
                t = jnp.tanh(a * (_GELU_C1 + _GELU_C2 * (a * a)))
                w_s[rows, cols] = g * (a * (1.0 + t)).astype(bf16)

    acc[...] += _dg(vt_ref[...], w_s[...])

    @pl.when(j == pl.num_programs(1) - 1)
    def _():
        gate2 = mod_ref[N_MOD - 1:N_MOD, :]
        out_ref[...] = x1_ref[...] + gate2 * acc[...].T


def _experts_call(h2b, f1, cnt, rk, f2, u_b, vt_b, x1, mod, tiles_per_batch):
    D, T = h2b.shape
    tm, te = TOK_TILE, EXP_TILE
    per_tok = pl.BlockSpec((PEER_HEADS, PEER_KEYS, tm), lambda i, j: (0, 0, i))
    return pl.pallas_call(
        _experts_kernel,
        grid=(T // tm, PEER_EXPERTS // te),
        in_specs=[
            pl.BlockSpec((D, tm), lambda i, j: (0, i)),
            per_tok, per_tok, per_tok, per_tok,
            pl.BlockSpec((te, D), lambda i, j: (j, 0)),
            pl.BlockSpec((D, te), lambda i, j: (0, j)),
            pl.BlockSpec((tm, D), lambda i, j: (i, 0)),
            pl.BlockSpec((None, N_MOD, D), lambda i, j: (i // tiles_per_batch, 0, 0)),
        ],
        out_specs=pl.BlockSpec((tm, D), lambda i, j: (i, 0)),
        out_shape=jax.ShapeDtypeStruct((T, D), f32),
        scratch_shapes=[
            pltpu.VMEM((D, tm), f32),
            pltpu.VMEM((te, tm), f32),
            pltpu.VMEM((te, tm), bf16),
            pltpu.VMEM((PEER_HEADS, PEER_KEYS, tm), bf16),
            pltpu.VMEM((PEER_HEADS, PEER_KEYS, tm), bf16),
        ],
        compiler_params=pltpu.CompilerParams(
            dimension_semantics=("arbitrary", "arbitrary"), vmem_limit_bytes=VMEM_LIMIT),
        name="experts",
    )(h2b, f1, cnt, rk, f2, u_b, vt_b, x1, mod)


def _proj_columns():
    half = np.arange(HALF_DIM)
    cols = []
    for c in range(N_Q_HEADS // 2):
        for part in range(2):
            for head in (2 * c, 2 * c + 1):
                cols.append(head * HEAD_DIM + part * HALF_DIM + half)
    for g in range(N_KV_HEADS):
        for part in (0, 0, 1, 1):
            cols.append(ATTN_WIDTH + g * HEAD_DIM + part * HALF_DIM + half)
    for g in range(N_KV_HEADS):
        for _ in range(2):
            cols.append(ATTN_WIDTH + KV_WIDTH + g * HEAD_DIM + np.arange(HEAD_DIM))
    cols.append(np.arange(ATTN_WIDTH + 2 * KV_WIDTH, ATTN_WIDTH + 2 * KV_WIDTH + 3 * CONV_WIDTH))
    return np.concatenate(cols)


_PROJ_COLS = _proj_columns()
_TILE_DIM = np.concatenate([np.arange(HALF_DIM), np.arange(HALF_DIM),
                            HALF_DIM + np.arange(HALF_DIM), HALF_DIM + np.arange(HALF_DIM)])
_TILE_HEAD = (np.arange(LANES) // HALF_DIM) % 2
_MSAME = (_TILE_HEAD[:, None] == _TILE_HEAD[None, :]).astype(np.float32)


def kernel(x, c, positions, ada_w, ada_b, norm1_g, w_in, q_norm_g, k_norm_g, sinks, conv_w,
           attn_out_g, conv_out_g, w_out, norm2_g, peer_wq, peer_subkeys, peer_u, peer_v):
    B, S, D = x.shape
    T = B * S
    assert ada_w.shape[0] == 1, "single-layer block"
    assert D == D_MODEL and S % SEQ_TILE == 0 and T % TOK_TILE == 0 and T % ROUTER_TILE == 0
    assert S % TOK_TILE == 0

    c_pad = jnp.pad(c, ((0, SUBLANES - B), (0, 0)))
    mod = _mod_call(c_pad, ada_w[0], ada_b[0][None, :])[:B].reshape(B, N_MOD, D)

    inv_freq = ROPE_THETA ** (-jnp.arange(0, HEAD_DIM, 2, dtype=f32) / HEAD_DIM)
    freq = inv_freq[_TILE_DIM % HALF_DIM][None, :]
    sgn = jnp.asarray(np.where(np.arange(LANES) < LANES // 2, -1.0, 1.0), f32)[None, :]
    win = w_in[0][:, _PROJ_COLS].astype(bf16)
    wout = w_out[0].astype(bf16)
    gq = jnp.tile(q_norm_g[0][_TILE_DIM], ATTN_WIDTH // LANES)[None, :]
    gk = jnp.tile(k_norm_g[0][_TILE_DIM], N_KV_HEADS)[None, :]
    x1, h2t = _mixer_call(
        sinks[0], x, positions.astype(f32)[..., None], mod, norm1_g, win, gq, gk, freq, sgn,
        jnp.asarray(_MSAME, bf16), conv_w[0], attn_out_g, conv_out_g, wout, norm2_g)

    wqh, wql = _split(peer_wq[0].T)
    skh, skl = _split(peer_subkeys[0].reshape(2 * PEER_HEADS, PEER_KEYS, PEER_QDIM // 2))
    f1, cnt, rk, f2, h2b = _router_call(h2t, wqh, wql, skh, skl)

    out = _experts_call(h2b, f1, cnt, rk, f2, peer_u[0].astype(bf16), peer_v[0].T.astype(bf16),
                        x1.reshape(T, D), mod, S // TOK_TILE)
    return out.reshape(B, S, D)
```

```python
import numpy as np
import jax
import jax.numpy as jnp
from jax import lax
from jax.experimental import pallas as pl
from jax.experimental.pallas import tpu as pltpu

D_MODEL = 1024
HEAD_DIM = 64
HALF_DIM = HEAD_DIM // 2
N_Q_HEADS = 8
N_KV_HEADS = 2
ATTN_WIDTH = N_Q_HEADS * HEAD_DIM
KV_WIDTH = N_KV_HEADS * HEAD_DIM
WINDOW = 128
ROPE_THETA = 10000.0
CONV_WIDTH = D_MODEL - ATTN_WIDTH
CONV_K = 3
PEER_HEADS = 8
PEER_KEYS = 128
PEER_EXPERTS = PEER_KEYS * PEER_KEYS
PEER_TOPK = 16
PEER_QDIM = 256
N_MOD = 6
EPS = 1e-6
NEG = -1e30

LANES = 128
SUBLANES = 8
BF16_ROWS = 2 * SUBLANES
N_ROW_GROUPS = PEER_KEYS // SUBLANES
N_KEY_VREGS = PEER_KEYS // BF16_ROWS
VMEM_LIMIT = 56 * 1024 * 1024

Q_OFF = 0
K_OFF = ATTN_WIDTH
V_OFF = K_OFF + 2 * KV_WIDTH
CB_OFF = V_OFF + 2 * KV_WIDTH
CC_OFF = CB_OFF + CONV_WIDTH
CU_OFF = CC_OFF + CONV_WIDTH
PROJ_WIDTH = CU_OFF + CONV_WIDTH

SEQ_TILE = 512
ROUTER_TILE = 512
TOK_TILE = 512
EXP_TILE = 1024

_NN = (((1,), (0,)), ((), ()))
_NT = (((1,), (1,)), ((), ()))

f32 = jnp.float32
bf16 = jnp.bfloat16


def _split(x):
    hi = x.astype(bf16)
    lo = (x - hi.astype(f32)).astype(bf16)
    return hi, lo


def _dg(a, b, dims=_NN):
    return lax.dot_general(a, b, dims, preferred_element_type=f32)


def _dot3(a, b, dims=_NN):
    return _dg(a[0], b[0], dims) + (_dg(a[0], b[1], dims) + _dg(a[1], b[0], dims))


def _rms(x):
    return x * lax.rsqrt(jnp.mean(x * x, axis=-1, keepdims=True) + EPS)


def _mod_kernel(c_ref, w_ref, b_ref, o_ref):
    c = c_ref[...]
    ca = c * (1.0 / (1.0 + jnp.exp(-c)))
    o_ref[...] = _dot3(_split(ca), _split(w_ref[...])) + b_ref[...]


def _mod_call(c_pad, ada_w, ada_b):
    nb = 1536
    n = ada_w.shape[1]
    return pl.pallas_call(
        _mod_kernel,
        grid=(n // nb,),
        in_specs=[
            pl.BlockSpec((SUBLANES, D_MODEL), lambda j: (0, 0)),
            pl.BlockSpec((D_MODEL, nb), lambda j: (0, j)),
            pl.BlockSpec((1, nb), lambda j: (0, j)),
        ],
        out_specs=pl.BlockSpec((SUBLANES, nb), lambda j: (0, j)),
        out_shape=jax.ShapeDtypeStruct((SUBLANES, n), f32),
        compiler_params=pltpu.CompilerParams(
            dimension_semantics=("arbitrary",), vmem_limit_bytes=VMEM_LIMIT),
        name="mod",
    )(c_pad, ada_w, ada_b)


def _mixer_kernel(sink_ref, x_ref, pos_ref, mod_ref, g1_ref, win_ref, gq_ref, gk_ref,
                  freq_ref, sgn_ref, msame_ref, convw_ref, ga_ref, gc_ref, wout_ref,
                  g2_ref, x1_ref, h2t_ref, qbuf, kbuf, vbuf, abuf, ucarry):
    ts = x_ref.shape[0]
    s_idx = pl.program_id(1)
    x = x_ref[...]
    mod = mod_ref[...]
    shift1, scale1, gate1 = mod[0:1], mod[1:2], mod[2:3]
    shift2, scale2 = mod[3:4], mod[4:5]

    h = _rms(x) * g1_ref[...]
    hb = (h * (1.0 + scale1) + shift1).astype(bf16)

    def proj(c0, c1):
        return _dg(hb, win_ref[:, c0:c1])

    ang = pos_ref[...] * freq_ref[...]
    cosf = jnp.cos(ang)
    sins = jnp.sin(ang) * sgn_ref[...]
    msame = msame_ref[...]

    def headnorm_rope(t, g):
        sq = _split(t * t)
        ss = _dg(sq[0], msame) + _dg(sq[1], msame)
        tn = t * lax.rsqrt(ss * (1.0 / HEAD_DIM) + EPS) * g
        return tn * cosf + pltpu.roll(tn, LANES // 2, 1) * sins

    @pl.when(s_idx == 0)
    def _():
        kbuf[0:WINDOW, :] = jnp.zeros((WINDOW, 2 * LANES), f32)
        vbuf[0:WINDOW, :] = jnp.zeros((WINDOW, 2 * LANES), f32)
        ucarry[...] = jnp.zeros(ucarry.shape, f32)

    for c in range(ATTN_WIDTH // LANES):
        sl = slice(c * LANES, (c + 1) * LANES)
        qbuf[:, sl] = headnorm_rope(proj(Q_OFF + c * LANES, Q_OFF + (c + 1) * LANES), gq_ref[:, sl])
    for g in range(N_KV_HEADS):
        sl = slice(g * LANES, (g + 1) * LANES)
        kbuf[WINDOW:WINDOW + ts, sl] = headnorm_rope(
            proj(K_OFF + g * LANES, K_OFF + (g + 1) * LANES), gk_ref[:, sl])
    vbuf[WINDOW:WINDOW + ts, :] = proj(V_OFF, V_OFF + 2 * LANES)

    qi = lax.broadcasted_iota(jnp.int32, (WINDOW, 2 * WINDOW), 0)
    kj = lax.broadcasted_iota(jnp.int32, (WINDOW, 2 * WINDOW), 1)
    band = (kj > qi) & (kj <= qi + WINDOW)
    lane = lax.broadcasted_iota(jnp.int32, (1, LANES), 1)

    def attn_block(n, carry):
        r0 = pl.multiple_of(n * WINDOW, WINDOW)
        qb = qbuf[pl.ds(r0, WINDOW), :]
        kb = kbuf[pl.ds(r0, 2 * WINDOW), :]
        vb = vbuf[pl.ds(r0, 2 * WINDOW), :]
        first = jnp.logical_and(s_idx == 0, n == 0)
        valid = band & (kj >= jnp.where(first, WINDOW, 0))
        for c in range(ATTN_WIDTH // LANES):
            g = c // 2
            qt = qb[:, c * LANES:(c + 1) * LANES]
            kt = kb[:, g * LANES:(g + 1) * LANES].astype(bf16)
            vt = vb[:, g * LANES:(g + 1) * LANES]
            o = None
            for sub in range(2):
                qm = jnp.where((lane // HALF_DIM) % 2 == sub, qt, 0.0).astype(bf16)
                s = _dg(qm, kt, _NT) * (HEAD_DIM ** -0.5)
                s = jnp.where(valid, s, NEG)
                sink = sink_ref[2 * c + sub]
                m = jnp.maximum(jnp.max(s, axis=-1, keepdims=True), sink)
                e = jnp.exp(s - m)
                den = jnp.sum(e, axis=-1, keepdims=True) + jnp.exp(sink - m)
                p = e / den
                vm = jnp.where(lane // HEAD_DIM == sub, vt, 0.0).astype(bf16)
                od = _dg(p.astype(bf16), vm)
                o = od if o is None else o + od
            abuf[pl.ds(r0, WINDOW), c * LANES:(c + 1) * LANES] = o
        return carry

    lax.fori_loop(0, ts // WINDOW, attn_block, 0)
    kbuf[0:WINDOW, :] = kbuf[ts:ts + WINDOW, :]
    vbuf[0:WINDOW, :] = vbuf[ts:ts + WINDOW, :]

    cb = proj(CB_OFF, CB_OFF + CONV_WIDTH)
    u = proj(CC_OFF, CC_OFF + CONV_WIDTH) * proj(CU_OFF, CU_OFF + CONV_WIDTH)
    rowi = lax.broadcasted_iota(jnp.int32, u.shape, 0)
    prev1 = ucarry[SUBLANES - 1:SUBLANES, :]
    prev2 = ucarry[SUBLANES - 2:SUBLANES - 1, :]
    u1 = jnp.where(rowi == 0, prev1, pltpu.roll(u, 1, 0))
    u2 = jnp.where(rowi == 0, prev2, jnp.where(rowi == 1, prev1, pltpu.roll(u, 2, 0)))
    ucarry[...] = u[ts - SUBLANES:ts, :]
    w = convw_ref[...]
    conv = cb * (w[0:1] * u2 + w[1:2] * u1 + w[2:3] * u)

    ra = (_rms(abuf[...]) * ga_ref[...]).astype(bf16)
    rc = (_rms(conv) * gc_ref[...]).astype(bf16)
    y = _dg(ra, wout_ref[0:ATTN_WIDTH, :]) + _dg(rc, wout_ref[ATTN_WIDTH:, :])
    x1 = x + gate1 * y
    x1_ref[...] = x1
    h2 = _rms(x1) * g2_ref[...]
    h2t_ref[...] = (h2 * (1.0 + scale2) + shift2).T


def _const_spec(shape):
    nd = len(shape)
    return pl.BlockSpec(shape, lambda *_: (0,) * nd, pipeline_mode=pl.Buffered(1))


def _mixer_call(sinks, x, pos, mod, g1, win, gq, gk, freq, sgn, msame, convw, ga, gc, wout, g2):
    B, S, D = x.shape
    ts = SEQ_TILE
    nst = S // ts
    return pl.pallas_call(
        _mixer_kernel,
        grid=(B, nst),
        in_specs=[
            pl.BlockSpec(memory_space=pltpu.SMEM),
            pl.BlockSpec((None, ts, D), lambda b, s: (b, s, 0)),
            pl.BlockSpec((None, ts, 1), lambda b, s: (b, s, 0)),
            pl.BlockSpec((None, N_MOD, D), lambda b, s: (b, 0, 0)),
            _const_spec(g1.shape), _const_spec(win.shape),
            _const_spec(gq.shape), _const_spec(gk.shape), _const_spec(freq.shape),
            _const_spec(sgn.shape), _const_spec(msame.shape), _const_spec(convw.shape),
            _const_spec(ga.shape), _const_spec(gc.shape), _const_spec(wout.shape),
            _const_spec(g2.shape),
        ],
        out_specs=[
            pl.BlockSpec((None, ts, D), lambda b, s: (b, s, 0)),
            pl.BlockSpec((D, ts), lambda b, s: (0, b * nst + s)),
        ],
        out_shape=[
            jax.ShapeDtypeStruct((B, S, D), f32),
            jax.ShapeDtypeStruct((D, B * S), f32),
        ],
        scratch_shapes=[
            pltpu.VMEM((ts, ATTN_WIDTH), f32),
            pltpu.VMEM((WINDOW + ts, 2 * LANES), f32),
            pltpu.VMEM((WINDOW + ts, 2 * LANES), f32),
            pltpu.VMEM((ts, ATTN_WIDTH), f32),
            pltpu.VMEM((SUBLANES, CONV_WIDTH), f32),
        ],
        compiler_params=pltpu.CompilerParams(
            dimension_semantics=("arbitrary", "arbitrary"), vmem_limit_bytes=VMEM_LIMIT),
        name="mixer",
    )(sinks, x, pos, mod, g1, win, gq, gk, freq, sgn, msame, convw, ga, gc, wout, g2)


def _oddeven_merge_sort_pairs(n):
    pairs = []
    p = 1
    while p < n:
        k = p
        while k >= 1:
            for j in range(k % p, n - k, 2 * k):
                for i in range(min(k, n - j - k)):
                    if (i + j) // (2 * p) == (i + j + k) // (2 * p):
                        pairs.append((i + j, i + j + k))
            k //= 2
        p *= 2
    return pairs


_SORT16 = _oddeven_merge_sort_pairs(PEER_TOPK)


def _bitonic_desc(z):
    z = list(z)
    d = PEER_TOPK // 2
    while d >= 1:
        for r in range(PEER_TOPK):
            if not r & d:
                hi, lo = jnp.maximum(z[r], z[r + d]), jnp.minimum(z[r], z[r + d])
                z[r], z[r + d] = hi, lo
        d //= 2
    return z


def _merge_top(R, L):
    z = list(R)
    for r in range(PEER_TOPK - len(L), PEER_TOPK):
        z[r] = jnp.maximum(R[r], L[PEER_TOPK - 1 - r])
    return _bitonic_desc(z)


def _top16_rows(sc):
    x = [sc[SUBLANES * g:SUBLANES * (g + 1), :] for g in range(PEER_KEYS // SUBLANES)]
    for i, j in _SORT16:
        x[i], x[j] = jnp.maximum(x[i], x[j]), jnp.minimum(x[i], x[j])
    for shift in (4, 2, 1):
        z = [jnp.maximum(x[r], pltpu.roll(x[PEER_TOPK - 1 - r], shift, 0)) for r in range(PEER_TOPK)]
        x = _bitonic_desc(z)
    return x


def _router_kernel(h2t_ref, wqh_ref, wql_ref, skh_ref, skl_ref,
                   rows_ref, gate_ref, h2b_ref, q_s, sc_s, top_s, row_s):
    tr = h2t_ref.shape[1]
    hs = _split(h2t_ref[...])
    h2b_ref[...] = hs[0]
    q_s[...] = _dot3((wqh_ref[...], wql_ref[...]), hs)
    top_s[...] = jnp.zeros(top_s.shape, f32)
    sub = lax.broadcasted_iota(jnp.int32, (SUBLANES, tr), 0)

    def head_body(h, carry):
        for p in range(2):
            hp = 2 * h + p
            r0 = pl.multiple_of(hp * PEER_KEYS, PEER_KEYS)
            sc = _dot3((skh_ref[hp], skl_ref[hp]), _split(q_s[pl.ds(r0, PEER_KEYS), :]))
            sc_s[hp] = sc
            top = _top16_rows(sc)
            for r in range(PEER_TOPK):
                top_s[p, r] = jnp.where(sub == h, top[r], top_s[p, r])
        return carry

    lax.fori_loop(0, PEER_HEADS, head_body, 0)

    a = [top_s[0, r] for r in range(PEER_TOPK)]
    b = [top_s[1, r] for r in range(PEER_TOPK)]
    R = [a[0] + b[r] for r in range(16)]
    R = _merge_top(R, [a[r] + b[0] for r in range(1, 16)])
    R = _merge_top(R, [a[1] + b[r] for r in range(1, 8)])
    R = _merge_top(R, [a[r] + b[1] for r in range(2, 8)])
    R = _merge_top(R, [a[2] + b[r] for r in range(2, 5)])
    R = _merge_top(R, [a[r] + b[2] for r in range(3, 5)])
    R = _merge_top(R, [a[3] + b[3]])
    tau = R[PEER_TOPK - 1]
    z = jnp.exp(R[0] - R[0])
    for r in range(1, PEER_TOPK):
        z = z + jnp.exp(R[r] - R[0])
    row_s[0] = a[0]
    row_s[1] = b[0]
    row_s[2] = 0.5 / z
    row_s[3] = tau

    def fac_body(h, carry):
        row = pl.ds(h, 1)
        s1 = sc_s[2 * h]
        s2 = sc_s[2 * h + 1]
        tau_h = row_s[3, row, :]
        cnt = jnp.zeros(s1.shape, f32)
        rk = jnp.zeros(s2.shape, f32)
        for r in range(PEER_TOPK):
            b_r = top_s[1, r, row, :]
            cnt = jnp.where(s1 + b_r >= tau_h, float(r + 1), cnt)
            rk = jnp.where(b_r > s2, float(r + 1), rk)
        f1 = jnp.exp(s1 - row_s[0, row, :]) * row_s[2, row, :]
        f2 = jnp.exp(s2 - row_s[1, row, :])
        for tc in range(tr // LANES):
            cols = slice(tc * LANES, (tc + 1) * LANES)
            for ig in range(N_ROW_GROUPS):
                keys = slice(ig * SUBLANES, (ig + 1) * SUBLANES)
                rows_ref[tc, ig, h, 0] = f1[keys, cols]
                rows_ref[tc, ig, h, 1] = cnt[keys, cols]
            for v in range(N_KEY_VREGS):
                keys = slice(v * BF16_ROWS, (v + 1) * BF16_ROWS)
                gate_ref[tc, v, h, 0] = rk[keys, cols]
                gate_ref[tc, v, h, 1] = f2[keys, cols]
        return carry

    lax.fori_loop(0, PEER_HEADS, fac_body, 0)


def _router_call(h2t, wqh, wql, skh, skl):
    D, T = h2t.shape
    tr = ROUTER_TILE
    rows_shape = (N_ROW_GROUPS, PEER_HEADS, 2, SUBLANES, LANES)
    gate_shape = (N_KEY_VREGS, PEER_HEADS, 2, BF16_ROWS, LANES)
    lead = lambda i: (i, 0, 0, 0, 0, 0)
    return pl.pallas_call(
        _router_kernel,
        grid=(T // tr,),
        in_specs=[
            pl.BlockSpec((D, tr), lambda i: (0, i)),
            _const_spec(wqh.shape), _const_spec(wql.shape),
            _const_spec(skh.shape), _const_spec(skl.shape),
        ],
        out_specs=[
            pl.BlockSpec((tr // LANES,) + rows_shape, lead),
            pl.BlockSpec((tr // LANES,) + gate_shape, lead),
            pl.BlockSpec((D, tr), lambda i: (0, i)),
        ],
        out_shape=[
            jax.ShapeDtypeStruct((T // LANES,) + rows_shape, f32),
            jax.ShapeDtypeStruct((T // LANES,) + gate_shape, f32),
            jax.ShapeDtypeStruct((D, T), bf16),
        ],
        scratch_shapes=[
            pltpu.VMEM((PEER_HEADS * PEER_QDIM, tr), f32),
            pltpu.VMEM((2 * PEER_HEADS, PEER_KEYS, tr), f32),
            pltpu.VMEM((2, PEER_TOPK, SUBLANES, tr), f32),
            pltpu.VMEM((4, SUBLANES, tr), f32),
        ],
        compiler_params=pltpu.CompilerParams(
            dimension_semantics=("arbitrary",), vmem_limit_bytes=VMEM_LIMIT),
        name="router",
    )(h2t, wqh, wql, skh, skl)


_GELU_C1 = float(np.sqrt(2.0 / np.pi))
_GELU_C2 = 0.044715 * _GELU_C1


def _experts_kernel(h2b_ref, rows_ref, gate_ref, u_ref, vt_ref, x1_ref, mod_ref,
                    out_ref, acc, a_s, w_s, gate_s):
    tm = h2b_ref.shape[1]
    te = u_ref.shape[0]
    assert te == SUBLANES * PEER_KEYS
    j = pl.program_id(1)

    @pl.when(j == 0)
    def _():
        acc[...] = jnp.zeros(acc.shape, f32)
        gate_s[...] = gate_ref[...].astype(bf16)

    a_s[:, 0:tm] = _dg(u_ref[...], h2b_ref[...])

    def packed_row(tc, h, kind, r):
        row = rows_ref[tc, j, h, kind][r:r + 1, :]
        return jnp.broadcast_to(row, (BF16_ROWS, LANES)).astype(bf16)

    for r in range(SUBLANES):
        for tc in range(tm // LANES):
            cols = slice(tc * LANES, (tc + 1) * LANES)
            f1 = [packed_row(tc, h, 0, r) for h in range(PEER_HEADS)]
            cnt = [packed_row(tc, h, 1, r) for h in range(PEER_HEADS)]
            for v in range(N_KEY_VREGS):
                rows = slice(r * PEER_KEYS + v * BF16_ROWS, r * PEER_KEYS + (v + 1) * BF16_ROWS)
                g = jnp.zeros((BF16_ROWS, LANES), bf16)
                for h in range(PEER_HEADS):
                    sel = jnp.where(gate_s[tc, v, h, 0] < cnt[h], gate_s[tc, v, h, 1],
                                    jnp.zeros((), bf16))
                    g = g + sel * f1[h]
                a = a_s[rows, cols]
                t = jnp.tanh(a * (_GELU_C1 + _GELU_C2 * (a * a)))
                w_s[rows, cols] = g * (a * (1.0 + t)).astype(bf16)

    acc[...] += _dg(vt_ref[...], w_s[:, 0:tm])

    @pl.when(j == pl.num_programs(1) - 1)
    def _():
        gate2 = mod_ref[N_MOD - 1:N_MOD, :]
        out_ref[...] = x1_ref[...] + gate2 * acc[...].T


def _experts_call(h2b, rows, gate, u_b, vt_b, x1, mod, tiles_per_batch):
    D, T = h2b.shape
    tm, te = TOK_TILE, EXP_TILE
    lead = lambda i, j: (i, 0, 0, 0, 0, 0)
    return pl.pallas_call(
        _experts_kernel,
        grid=(T // tm, PEER_EXPERTS // te),
        in_specs=[
            pl.BlockSpec((D, tm), lambda i, j: (0, i)),
            pl.BlockSpec((tm // LANES,) + rows.shape[1:], lead),
            pl.BlockSpec((tm // LANES,) + gate.shape[1:], lead),
            pl.BlockSpec((te, D), lambda i, j: (j, 0)),
            pl.BlockSpec((D, te), lambda i, j: (0, j)),
            pl.BlockSpec((tm, D), lambda i, j: (i, 0)),
            pl.BlockSpec((None, N_MOD, D), lambda i, j: (i // tiles_per_batch, 0, 0)),
        ],
        out_specs=pl.BlockSpec((tm, D), lambda i, j: (i, 0)),
        out_shape=jax.ShapeDtypeStruct((T, D), f32),
        scratch_shapes=[
            pltpu.VMEM((D, tm), f32),
            pltpu.VMEM((te, tm + LANES), f32),
            pltpu.VMEM((te, tm + LANES), bf16),
            pltpu.VMEM((tm // LANES,) + gate.shape[1:], bf16),
        ],
        compiler_params=pltpu.CompilerParams(
            dimension_semantics=("arbitrary", "arbitrary"), vmem_limit_bytes=VMEM_LIMIT),
        name="experts",
    )(h2b, rows, gate, u_b, vt_b, x1, mod)


def _proj_columns():
    half = np.arange(HALF_DIM)
    cols = []
    for c in range(N_Q_HEADS // 2):
        for part in range(2):
            for head in (2 * c, 2 * c + 1):
                cols.append(head * HEAD_DIM + part * HALF_DIM + half)
    for g in range(N_KV_HEADS):
        for part in (0, 0, 1, 1):
            cols.append(ATTN_WIDTH + g * HEAD_DIM + part * HALF_DIM + half)
    for g in range(N_KV_HEADS):
        for _ in range(2):
            cols.append(ATTN_WIDTH + KV_WIDTH + g * HEAD_DIM + np.arange(HEAD_DIM))
    cols.append(np.arange(ATTN_WIDTH + 2 * KV_WIDTH, ATTN_WIDTH + 2 * KV_WIDTH + 3 * CONV_WIDTH))
    return np.concatenate(cols)


_PROJ_COLS = _proj_columns()
_TILE_DIM = np.concatenate([np.arange(HALF_DIM), np.arange(HALF_DIM),
                            HALF_DIM + np.arange(HALF_DIM), HALF_DIM + np.arange(HALF_DIM)])
_TILE_HEAD = (np.arange(LANES) // HALF_DIM) % 2
_MSAME = (_TILE_HEAD[:, None] == _TILE_HEAD[None, :]).astype(np.float32)


def kernel(x, c, positions, ada_w, ada_b, norm1_g, w_in, q_norm_g, k_norm_g, sinks, conv_w,
           attn_out_g, conv_out_g, w_out, norm2_g, peer_wq, peer_subkeys, peer_u, peer_v):
    B, S, D = x.shape
    T = B * S
    assert ada_w.shape[0] == 1, "single-layer block"
    assert D == D_MODEL and S % SEQ_TILE == 0 and T % TOK_TILE == 0 and T % ROUTER_TILE == 0
    assert S % TOK_TILE == 0

    c_pad = jnp.pad(c, ((0, SUBLANES - B), (0, 0)))
    mod = _mod_call(c_pad, ada_w[0], ada_b[0][None, :])[:B].reshape(B, N_MOD, D)

    inv_freq = ROPE_THETA ** (-jnp.arange(0, HEAD_DIM, 2, dtype=f32) / HEAD_DIM)
    freq = inv_freq[_TILE_DIM % HALF_DIM][None, :]
    sgn = jnp.asarray(np.where(np.arange(LANES) < LANES // 2, -1.0, 1.0), f32)[None, :]
    win = w_in[0][:, _PROJ_COLS].astype(bf16)
    wout = w_out[0].astype(bf16)
    gq = jnp.tile(q_norm_g[0][_TILE_DIM], ATTN_WIDTH // LANES)[None, :]
    gk = jnp.tile(k_norm_g[0][_TILE_DIM], N_KV_HEADS)[None, :]
    x1, h2t = _mixer_call(
        sinks[0], x, positions.astype(f32)[..., None], mod, norm1_g, win, gq, gk, freq, sgn,
        jnp.asarray(_MSAME, bf16), conv_w[0], attn_out_g, conv_out_g, wout, norm2_g)

    wqh, wql = _split(peer_wq[0].T)
    skh, skl = _split(peer_subkeys[0].reshape(2 * PEER_HEADS, PEER_KEYS, PEER_QDIM // 2))
    rows, gate, h2b = _router_call(h2t, wqh, wql, skh, skl)

    out = _experts_call(h2b, rows, gate, peer_u[0].astype(bf16), peer_v[0].T.astype(bf16),
                        x1.reshape(T, D), mod, S // TOK_TILE)
    return out.reshape(B, S, D)
```

```python
import numpy as np
import jax
import jax.numpy as jnp
from jax import lax
from jax.experimental import pallas as pl
from jax.experimental.pallas import tpu as pltpu

D_MODEL = 1024
HEAD_DIM = 64
HALF_DIM = HEAD_DIM // 2
N_Q_HEADS = 8
N_KV_HEADS = 2
ATTN_WIDTH = N_Q_HEADS * HEAD_DIM
KV_WIDTH = N_KV_HEADS * HEAD_DIM
WINDOW = 128
ROPE_THETA = 10000.0
CONV_WIDTH = D_MODEL - ATTN_WIDTH
CONV_K = 3
PEER_HEADS = 8
PEER_KEYS = 128
PEER_EXPERTS = PEER_KEYS * PEER_KEYS
PEER_TOPK = 16
PEER_QDIM = 256
N_MOD = 6
EPS = 1e-6
NEG = -1e30

LANES = 128
SUBLANES = 8
BF16_ROWS = 2 * SUBLANES
N_ROW_GROUPS = PEER_KEYS // SUBLANES
N_KEY_VREGS = PEER_KEYS // BF16_ROWS
VMEM_LIMIT = 56 * 1024 * 1024

Q_OFF = 0
K_OFF = ATTN_WIDTH
V_OFF = K_OFF + 2 * KV_WIDTH
CB_OFF = V_OFF + 2 * KV_WIDTH
CC_OFF = CB_OFF + CONV_WIDTH
CU_OFF = CC_OFF + CONV_WIDTH
PROJ_WIDTH = CU_OFF + CONV_WIDTH

SEQ_TILE = 512
ROUTER_TILE = 512
TOK_TILE = 512
EXP_TILE = 1024

_NN = (((1,), (0,)), ((), ()))
_NT = (((1,), (1,)), ((), ()))

f32 = jnp.float32
bf16 = jnp.bfloat16


def _split(x):
    hi = x.astype(bf16)
    lo = (x - hi.astype(f32)).astype(bf16)
    return hi, lo


def _dg(a, b, dims=_NN):
    return lax.dot_general(a, b, dims, preferred_element_type=f32)


def _dot3(a, b, dims=_NN):
    return _dg(a[0], b[0], dims) + (_dg(a[0], b[1], dims) + _dg(a[1], b[0], dims))


def _rms(x):
    return x * lax.rsqrt(jnp.mean(x * x, axis=-1, keepdims=True) + EPS)


def _mod_kernel(c_ref, w_ref, b_ref, o_ref):
    c = c_ref[...]
    ca = c * (1.0 / (1.0 + jnp.exp(-c)))
    o_ref[...] = _dot3(_split(ca), _split(w_ref[...])) + b_ref[...]


def _mod_call(c_pad, ada_w, ada_b):
    nb = 1536
    n = ada_w.shape[1]
    return pl.pallas_call(
        _mod_kernel,
        grid=(n // nb,),
        in_specs=[
            pl.BlockSpec((SUBLANES, D_MODEL), lambda j: (0, 0)),
            pl.BlockSpec((D_MODEL, nb), lambda j: (0, j)),
            pl.BlockSpec((1, nb), lambda j: (0, j)),
        ],
        out_specs=pl.BlockSpec((SUBLANES, nb), lambda j: (0, j)),
        out_shape=jax.ShapeDtypeStruct((SUBLANES, n), f32),
        compiler_params=pltpu.CompilerParams(
            dimension_semantics=("arbitrary",), vmem_limit_bytes=VMEM_LIMIT),
        name="mod",
    )(c_pad, ada_w, ada_b)


def _mixer_kernel(sink_ref, x_ref, pos_ref, mod_ref, g1_ref, win_ref, gq_ref, gk_ref,
                  freq_ref, sgn_ref, msame_ref, convw_ref, ga_ref, gc_ref, wout_ref,
                  g2_ref, x1_ref, h2t_ref, qbuf, kbuf, vbuf, abuf, ucarry):
    ts = x_ref.shape[0]
    s_idx = pl.program_id(1)
    x = x_ref[...]
    mod = mod_ref[...]
    shift1, scale1, gate1 = mod[0:1], mod[1:2], mod[2:3]
    shift2, scale2 = mod[3:4], mod[4:5]

    h = _rms(x) * g1_ref[...]
    hb = (h * (1.0 + scale1) + shift1).astype(bf16)

    def proj(c0, c1):
        return _dg(hb, win_ref[:, c0:c1])

    ang = pos_ref[...] * freq_ref[...]
    cosf = jnp.cos(ang)
    sins = jnp.sin(ang) * sgn_ref[...]
    msame = msame_ref[...]

    def headnorm_rope(t, g):
        sq = _split(t * t)
        ss = _dg(sq[0], msame) + _dg(sq[1], msame)
        tn = t * lax.rsqrt(ss * (1.0 / HEAD_DIM) + EPS) * g
        return tn * cosf + pltpu.roll(tn, LANES // 2, 1) * sins

    @pl.when(s_idx == 0)
    def _():
        kbuf[0:WINDOW, :] = jnp.zeros((WINDOW, 2 * LANES), f32)
        vbuf[0:WINDOW, :] = jnp.zeros((WINDOW, 2 * LANES), f32)
        ucarry[...] = jnp.zeros(ucarry.shape, f32)

    for c in range(ATTN_WIDTH // LANES):
        sl = slice(c * LANES, (c + 1) * LANES)
        qbuf[:, sl] = headnorm_rope(proj(Q_OFF + c * LANES, Q_OFF + (c + 1) * LANES), gq_ref[:, sl])
    for g in range(N_KV_HEADS):
        sl = slice(g * LANES, (g + 1) * LANES)
        kbuf[WINDOW:WINDOW + ts, sl] = headnorm_rope(
            proj(K_OFF + g * LANES, K_OFF + (g + 1) * LANES), gk_ref[:, sl])
    vbuf[WINDOW:WINDOW + ts, :] = proj(V_OFF, V_OFF + 2 * LANES)

    qi = lax.broadcasted_iota(jnp.int32, (WINDOW, 2 * WINDOW), 0)
    kj = lax.broadcasted_iota(jnp.int32, (WINDOW, 2 * WINDOW), 1)
    band = (kj > qi) & (kj <= qi + WINDOW)
    lane = lax.broadcasted_iota(jnp.int32, (1, LANES), 1)

    def attn_block(n, carry):
        r0 = pl.multiple_of(n * WINDOW, WINDOW)
        qb = qbuf[pl.ds(r0, WINDOW), :]
        kb = kbuf[pl.ds(r0, 2 * WINDOW), :]
        vb = vbuf[pl.ds(r0, 2 * WINDOW), :]
        first = jnp.logical_and(s_idx == 0, n == 0)
        valid = band & (kj >= jnp.where(first, WINDOW, 0))
        for c in range(ATTN_WIDTH // LANES):
            g = c // 2
            qt = qb[:, c * LANES:(c + 1) * LANES]
            kt = kb[:, g * LANES:(g + 1) * LANES].astype(bf16)
            vt = vb[:, g * LANES:(g + 1) * LANES]
            o = None
            for sub in range(2):
                qm = jnp.where((lane // HALF_DIM) % 2 == sub, qt, 0.0).astype(bf16)
                s = _dg(qm, kt, _NT) * (HEAD_DIM ** -0.5)
                s = jnp.where(valid, s, NEG)
                sink = sink_ref[2 * c + sub]
                m = jnp.maximum(jnp.max(s, axis=-1, keepdims=True), sink)
                e = jnp.exp(s - m)
                den = jnp.sum(e, axis=-1, keepdims=True) + jnp.exp(sink - m)
                p = e / den
                vm = jnp.where(lane // HEAD_DIM == sub, vt, 0.0).astype(bf16)
                od = _dg(p.astype(bf16), vm)
                o = od if o is None else o + od
            abuf[pl.ds(r0, WINDOW), c * LANES:(c + 1) * LANES] = o
        return carry

    lax.fori_loop(0, ts // WINDOW, attn_block, 0)
    kbuf[0:WINDOW, :] = kbuf[ts:ts + WINDOW, :]
    vbuf[0:WINDOW, :] = vbuf[ts:ts + WINDOW, :]

    cb = proj(CB_OFF, CB_OFF + CONV_WIDTH)
    u = proj(CC_OFF, CC_OFF + CONV_WIDTH) * proj(CU_OFF, CU_OFF + CONV_WIDTH)
    rowi = lax.broadcasted_iota(jnp.int32, u.shape, 0)
    prev1 = ucarry[SUBLANES - 1:SUBLANES, :]
    prev2 = ucarry[SUBLANES - 2:SUBLANES - 1, :]
    u1 = jnp.where(rowi == 0, prev1, pltpu.roll(u, 1, 0))
    u2 = jnp.where(rowi == 0, prev2, jnp.where(rowi == 1, prev1, pltpu.roll(u, 2, 0)))
    ucarry[...] = u[ts - SUBLANES:ts, :]
    w = convw_ref[...]
    conv = cb * (w[0:1] * u2 + w[1:2] * u1 + w[2:3] * u)

    ra = (_rms(abuf[...]) * ga_ref[...]).astype(bf16)
    rc = (_rms(conv) * gc_ref[...]).astype(bf16)
    y = _dg(ra, wout_ref[0:ATTN_WIDTH, :]) + _dg(rc, wout_ref[ATTN_WIDTH:, :])
    x1 = x + gate1 * y
    x1_ref[...] = x1
    h2 = _rms(x1) * g2_ref[...]
    h2t_ref[...] = (h2 * (1.0 + scale2) + shift2).T


def _const_spec(shape):
    nd = len(shape)
    return pl.BlockSpec(shape, lambda *_: (0,) * nd, pipeline_mode=pl.Buffered(1))


def _mixer_call(sinks, x, pos, mod, g1, win, gq, gk, freq, sgn, msame, convw, ga, gc, wout, g2):
    B, S, D = x.shape
    ts = SEQ_TILE
    nst = S // ts
    return pl.pallas_call(
        _mixer_kernel,
        grid=(B, nst),
        in_specs=[
            pl.BlockSpec(memory_space=pltpu.SMEM),
            pl.BlockSpec((None, ts, D), lambda b, s: (b, s, 0)),
            pl.BlockSpec((None, ts, 1), lambda b, s: (b, s, 0)),
            pl.BlockSpec((None, N_MOD, D), lambda b, s: (b, 0, 0)),
            _const_spec(g1.shape), _const_spec(win.shape),
            _const_spec(gq.shape), _const_spec(gk.shape), _const_spec(freq.shape),
            _const_spec(sgn.shape), _const_spec(msame.shape), _const_spec(convw.shape),
            _const_spec(ga.shape), _const_spec(gc.shape), _const_spec(wout.shape),
            _const_spec(g2.shape),
        ],
        out_specs=[
            pl.BlockSpec((None, ts, D), lambda b, s: (b, s, 0)),
            pl.BlockSpec((D, ts), lambda b, s: (0, b * nst + s)),
        ],
        out_shape=[
            jax.ShapeDtypeStruct((B, S, D), f32),
            jax.ShapeDtypeStruct((D, B * S), f32),
        ],
        scratch_shapes=[
            pltpu.VMEM((ts, ATTN_WIDTH), f32),
            pltpu.VMEM((WINDOW + ts, 2 * LANES), f32),
            pltpu.VMEM((WINDOW + ts, 2 * LANES), f32),
            pltpu.VMEM((ts, ATTN_WIDTH), f32),
            pltpu.VMEM((SUBLANES, CONV_WIDTH), f32),
        ],
        compiler_params=pltpu.CompilerParams(
            dimension_semantics=("arbitrary", "arbitrary"), vmem_limit_bytes=VMEM_LIMIT),
        name="mixer",
    )(sinks, x, pos, mod, g1, win, gq, gk, freq, sgn, msame, convw, ga, gc, wout, g2)


def _oddeven_merge_sort_pairs(n):
    pairs = []
    p = 1
    while p < n:
        k = p
        while k >= 1:
            for j in range(k % p, n - k, 2 * k):
                for i in range(min(k, n - j - k)):
                    if (i + j) // (2 * p) == (i + j + k) // (2 * p):
                        pairs.append((i + j, i + j + k))
            k //= 2
        p *= 2
    return pairs


_SORT16 = _oddeven_merge_sort_pairs(PEER_TOPK)


def _bitonic_desc(z):
    z = list(z)
    d = PEER_TOPK // 2
    while d >= 1:
        for r in range(PEER_TOPK):
            if not r & d:
                hi, lo = jnp.maximum(z[r], z[r + d]), jnp.minimum(z[r], z[r + d])
                z[r], z[r + d] = hi, lo
        d //= 2
    return z


def _merge_top(R, L):
    z = list(R)
    for r in range(PEER_TOPK - len(L), PEER_TOPK):
        z[r] = jnp.maximum(R[r], L[PEER_TOPK - 1 - r])
    return _bitonic_desc(z)


def _top16_rows(sc):
    x = [sc[SUBLANES * g:SUBLANES * (g + 1), :] for g in range(PEER_KEYS // SUBLANES)]
    for i, j in _SORT16:
        x[i], x[j] = jnp.maximum(x[i], x[j]), jnp.minimum(x[i], x[j])
    for shift in (4, 2, 1):
        z = [jnp.maximum(x[r], pltpu.roll(x[PEER_TOPK - 1 - r], shift, 0)) for r in range(PEER_TOPK)]
        x = _bitonic_desc(z)
    return x


def _router_kernel(h2t_ref, wqh_ref, wql_ref, skh_ref, skl_ref,
                   rows_ref, gate_ref, h2b_ref, q_s, sc_s, top_s, row_s):
    tr = h2t_ref.shape[1]
    hs = _split(h2t_ref[...])
    h2b_ref[...] = hs[0]
    q_s[...] = _dot3((wqh_ref[...], wql_ref[...]), hs)
    top_s[...] = jnp.zeros(top_s.shape, f32)
    sub = lax.broadcasted_iota(jnp.int32, (SUBLANES, tr), 0)

    def head_body(h, carry):
        for p in range(2):
            hp = 2 * h + p
            r0 = pl.multiple_of(hp * PEER_KEYS, PEER_KEYS)
            sc = _dot3((skh_ref[hp], skl_ref[hp]), _split(q_s[pl.ds(r0, PEER_KEYS), :]))
            sc_s[hp] = sc
            top = _top16_rows(sc)
            for r in range(PEER_TOPK):
                top_s[p, r] = jnp.where(sub == h, top[r], top_s[p, r])
        return carry

    lax.fori_loop(0, PEER_HEADS, head_body, 0)

    a = [top_s[0, r] for r in range(PEER_TOPK)]
    b = [top_s[1, r] for r in range(PEER_TOPK)]
    R = [a[0] + b[r] for r in range(16)]
    R = _merge_top(R, [a[r] + b[0] for r in range(1, 16)])
    R = _merge_top(R, [a[1] + b[r] for r in range(1, 8)])
    R = _merge_top(R, [a[r] + b[1] for r in range(2, 8)])
    R = _merge_top(R, [a[2] + b[r] for r in range(2, 5)])
    R = _merge_top(R, [a[r] + b[2] for r in range(3, 5)])
    R = _merge_top(R, [a[3] + b[3]])
    tau = R[PEER_TOPK - 1]
    z = jnp.exp(R[0] - R[0])
    for r in range(1, PEER_TOPK):
        z = z + jnp.exp(R[r] - R[0])
    row_s[0] = a[0]
    row_s[1] = b[0]
    row_s[2] = 0.5 / z
    row_s[3] = tau

    def fac_body(h, carry):
        row = pl.ds(h, 1)
        s1 = sc_s[2 * h]
        s2 = sc_s[2 * h + 1]
        tau_h = row_s[3, row, :]
        cnt = jnp.zeros(s1.shape, f32)
        rk = jnp.zeros(s2.shape, f32)
        for r in range(PEER_TOPK):
            b_r = top_s[1, r, row, :]
            cnt = jnp.where(s1 + b_r >= tau_h, float(r + 1), cnt)
            rk = jnp.where(b_r > s2, float(r + 1), rk)
        f1 = jnp.exp(s1 - row_s[0, row, :]) * row_s[2, row, :]
        f2 = jnp.exp(s2 - row_s[1, row, :])
        for tc in range(tr // LANES):
            cols = slice(tc * LANES, (tc + 1) * LANES)
            for ig in range(N_ROW_GROUPS):
                keys = slice(ig * SUBLANES, (ig + 1) * SUBLANES)
                rows_ref[tc, ig, h, 0] = f1[keys, cols]
                rows_ref[tc, ig, h, 1] = cnt[keys, cols]
            for v in range(N_KEY_VREGS):
                keys = slice(v * BF16_ROWS, (v + 1) * BF16_ROWS)
                gate_ref[tc, v, h, 0] = rk[keys, cols]
                gate_ref[tc, v, h, 1] = f2[keys, cols]
        return carry

    lax.fori_loop(0, PEER_HEADS, fac_body, 0)


def _router_call(h2t, wqh, wql, skh, skl):
    D, T = h2t.shape
    tr = ROUTER_TILE
    rows_shape = (N_ROW_GROUPS, PEER_HEADS, 2, SUBLANES, LANES)
    gate_shape = (N_KEY_VREGS, PEER_HEADS, 2, BF16_ROWS, LANES)
    lead = lambda i: (i, 0, 0, 0, 0, 0)
    return pl.pallas_call(
        _router_kernel,
        grid=(T // tr,),
        in_specs=[
            pl.BlockSpec((D, tr), lambda i: (0, i)),
            _const_spec(wqh.shape), _const_spec(wql.shape),
            _const_spec(skh.shape), _const_spec(skl.shape),
        ],
        out_specs=[
            pl.BlockSpec((tr // LANES,) + rows_shape, lead),
            pl.BlockSpec((tr // LANES,) + gate_shape, lead),
            pl.BlockSpec((D, tr), lambda i: (0, i)),
        ],
        out_shape=[
            jax.ShapeDtypeStruct((T // LANES,) + rows_shape, f32),
            jax.ShapeDtypeStruct((T // LANES,) + gate_shape, f32),
            jax.ShapeDtypeStruct((D, T), bf16),
        ],
        scratch_shapes=[
            pltpu.VMEM((PEER_HEADS * PEER_QDIM, tr), f32),
            pltpu.VMEM((2 * PEER_HEADS, PEER_KEYS, tr), f32),
            pltpu.VMEM((2, PEER_TOPK, SUBLANES, tr), f32),
            pltpu.VMEM((4, SUBLANES, tr), f32),
        ],
        compiler_params=pltpu.CompilerParams(
            dimension_semantics=("arbitrary",), vmem_limit_bytes=VMEM_LIMIT),
        name="router",
    )(h2t, wqh, wql, skh, skl)


_GELU_C1 = float(np.sqrt(2.0 / np.pi))
_GELU_C2 = 0.044715 * _GELU_C1


def _experts_kernel(h2b_ref, rows_ref, gate_ref, u_ref, vt_ref, x1_ref, mod_ref,
                    out_ref, acc, a_s, w_s, gate_s):
    tm = h2b_ref.shape[1]
    te = u_ref.shape[0]
    assert te == SUBLANES * PEER_KEYS
    j = pl.program_id(1)

    @pl.when(j == 0)
    def _():
        acc[...] = jnp.zeros(acc.shape, f32)
        gate_s[...] = gate_ref[...].astype(bf16)

    a_s[:, 0:tm] = _dg(u_ref[...], h2b_ref[...])

    def packed(row):
        return jnp.broadcast_to(row, (BF16_ROWS, LANES)).astype(bf16)

    zero = jnp.zeros((), bf16)
    for rp in range(SUBLANES // 2):
        def lane_tile(tc, carry, rp=rp):
            cols = pl.ds(pl.multiple_of(tc * LANES, LANES), LANES)
            f1, cnt = [], []
            for h in range(PEER_HEADS):
                f1_t, cnt_t = rows_ref[tc, j, h, 0], rows_ref[tc, j, h, 1]
                f1.append([packed(f1_t[2 * rp + k:2 * rp + k + 1, :]) for k in range(2)])
                cnt.append([packed(cnt_t[2 * rp + k:2 * rp + k + 1, :]) for k in range(2)])
            for v in range(N_KEY_VREGS):
                g = [jnp.zeros((BF16_ROWS, LANES), bf16) for _ in range(2)]
                for h in range(PEER_HEADS):
                    rk, f2 = gate_s[tc, v, h, 0], gate_s[tc, v, h, 1]
                    for k in range(2):
                        g[k] = g[k] + jnp.where(rk < cnt[h][k], f2, zero) * f1[h][k]
                for k in range(2):
                    r0 = (2 * rp + k) * PEER_KEYS + v * BF16_ROWS
                    a = a_s[r0:r0 + BF16_ROWS, cols]
                    t = jnp.tanh(a * (_GELU_C1 + _GELU_C2 * (a * a)))
                    w_s[r0:r0 + BF16_ROWS, cols] = g[k] * (a * (1.0 + t)).astype(bf16)
            return carry

        lax.fori_loop(0, tm // LANES, lane_tile, 0)

    acc[...] += _dg(vt_ref[...], w_s[:, 0:tm])

    @pl.when(j == pl.num_programs(1) - 1)
    def _():
        gate2 = mod_ref[N_MOD - 1:N_MOD, :]
        out_ref[...] = x1_ref[...] + gate2 * acc[...].T


def _experts_call(h2b, rows, gate, u_b, vt_b, x1, mod, tiles_per_batch):
    D, T = h2b.shape
    tm, te = TOK_TILE, EXP_TILE
    lead = lambda i, j: (i, 0, 0, 0, 0, 0)
    return pl.pallas_call(
        _experts_kernel,
        grid=(T // tm, PEER_EXPERTS // te),
        in_specs=[
            pl.BlockSpec((D, tm), lambda i, j: (0, i)),
            pl.BlockSpec((tm // LANES,) + rows.shape[1:], lead),
            pl.BlockSpec((tm // LANES,) + gate.shape[1:], lead),
            pl.BlockSpec((te, D), lambda i, j: (j, 0)),
            pl.BlockSpec((D, te), lambda i, j: (0, j)),
            pl.BlockSpec((tm, D), lambda i, j: (i, 0)),
            pl.BlockSpec((None, N_MOD, D), lambda i, j: (i // tiles_per_batch, 0, 0)),
        ],
        out_specs=pl.BlockSpec((tm, D), lambda i, j: (i, 0)),
        out_shape=jax.ShapeDtypeStruct((T, D), f32),
        scratch_shapes=[
            pltpu.VMEM((D, tm), f32),
            pltpu.VMEM((te, tm + LANES), f32),
            pltpu.VMEM((te, tm + LANES), bf16),
            pltpu.VMEM((tm // LANES,) + gate.shape[1:], bf16),
        ],
        compiler_params=pltpu.CompilerParams(
            dimension_semantics=("arbitrary", "arbitrary"), vmem_limit_bytes=VMEM_LIMIT),
        name="experts",
    )(h2b, rows, gate, u_b, vt_b, x1, mod)


def _proj_columns():
    half = np.arange(HALF_DIM)
    cols = []
    for c in range(N_Q_HEADS // 2):
        for part in range(2):
            for head in (2 * c, 2 * c + 1):
                cols.append(head * HEAD_DIM + part * HALF_DIM + half)
    for g in range(N_KV_HEADS):
        for part in (0, 0, 1, 1):
            cols.append(ATTN_WIDTH + g * HEAD_DIM + part * HALF_DIM + half)
    for g in range(N_KV_HEADS):
        for _ in range(2):
            cols.append(ATTN_WIDTH + KV_WIDTH + g * HEAD_DIM + np.arange(HEAD_DIM))
    cols.append(np.arange(ATTN_WIDTH + 2 * KV_WIDTH, ATTN_WIDTH + 2 * KV_WIDTH + 3 * CONV_WIDTH))
    return np.concatenate(cols)


_PROJ_COLS = _proj_columns()
_TILE_DIM = np.concatenate([np.arange(HALF_DIM), np.arange(HALF_DIM),
                            HALF_DIM + np.arange(HALF_DIM), HALF_DIM + np.arange(HALF_DIM)])
_TILE_HEAD = (np.arange(LANES) // HALF_DIM) % 2
_MSAME = (_TILE_HEAD[:, None] == _TILE_HEAD[None, :]).astype(np.float32)


def kernel(x, c, positions, ada_w, ada_b, norm1_g, w_in, q_norm_g, k_norm_g, sinks, conv_w,
           attn_out_g, conv_out_g, w_out, norm2_g, peer_wq, peer_subkeys, peer_u, peer_v):
    B, S, D = x.shape
    T = B * S
    assert ada_w.shape[0] == 1, "single-layer block"
    assert D == D_MODEL and S % SEQ_TILE == 0 and T % TOK_TILE == 0 and T % ROUTER_TILE == 0
    assert S % TOK_TILE == 0

    c_pad = jnp.pad(c, ((0, SUBLANES - B), (0, 0)))
    mod = _mod_call(c_pad, ada_w[0], ada_b[0][None, :])[:B].reshape(B, N_MOD, D)

    inv_freq = ROPE_THETA ** (-jnp.arange(0, HEAD_DIM, 2, dtype=f32) / HEAD_DIM)
    freq = inv_freq[_TILE_DIM % HALF_DIM][None, :]
    sgn = jnp.asarray(np.where(np.arange(LANES) < LANES // 2, -1.0, 1.0), f32)[None, :]
    win = w_in[0][:, _PROJ_COLS].astype(bf16)
    wout = w_out[0].astype(bf16)
    gq = jnp.tile(q_norm_g[0][_TILE_DIM], ATTN_WIDTH // LANES)[None, :]
    gk = jnp.tile(k_norm_g[0][_TILE_DIM], N_KV_HEADS)[None, :]
    x1, h2t = _mixer_call(
        sinks[0], x, positions.astype(f32)[..., None], mod, norm1_g, win, gq, gk, freq, sgn,
        jnp.asarray(_MSAME, bf16), conv_w[0], attn_out_g, conv_out_g, wout, norm2_g)

    wqh, wql = _split(peer_wq[0].T)
    skh, skl = _split(peer_subkeys[0].reshape(2 * PEER_HEADS, PEER_KEYS, PEER_QDIM // 2))
    rows, gate, h2b = _router_call(h2t, wqh, wql, skh, skl)

    out = _experts_call(h2b, rows, gate, peer_u[0].astype(bf16), peer_v[0].T.astype(bf16),
                        x1.reshape(T, D), mod, S // TOK_TILE)
    return out.reshape(B, S, D)
```

```python
import numpy as np
import jax
import jax.numpy as jnp
from jax import lax
from jax.experimental import pallas as pl
from jax.experimental.pallas import tpu as pltpu

D_MODEL = 1024
HEAD_DIM = 64
HALF_DIM = HEAD_DIM // 2
N_Q_HEADS = 8
N_KV_HEADS = 2
ATTN_WIDTH = N_Q_HEADS * HEAD_DIM
KV_WIDTH = N_KV_HEADS * HEAD_DIM
WINDOW = 128
ROPE_THETA = 10000.0
CONV_WIDTH = D_MODEL - ATTN_WIDTH
CONV_K = 3
PEER_HEADS = 8
PEER_KEYS = 128
PEER_EXPERTS = PEER_KEYS * PEER_KEYS
PEER_TOPK = 16
PEER_QDIM = 256
N_MOD = 6
EPS = 1e-6
NEG = -1e30

LANES = 128
SUBLANES = 8
BF16_ROWS = 2 * SUBLANES
N_ROW_GROUPS = PEER_KEYS // SUBLANES
N_KEY_VREGS = PEER_KEYS // BF16_ROWS
VMEM_LIMIT = 56 * 1024 * 1024

Q_OFF = 0
K_OFF = ATTN_WIDTH
V_OFF = K_OFF + 2 * KV_WIDTH
CB_OFF = V_OFF + 2 * KV_WIDTH
CC_OFF = CB_OFF + CONV_WIDTH
CU_OFF = CC_OFF + CONV_WIDTH
PROJ_WIDTH = CU_OFF + CONV_WIDTH

SEQ_TILE = 512
ROUTER_TILE = 512
TOK_TILE = 512
EXP_TILE = 1024

_NN = (((1,), (0,)), ((), ()))
_NT = (((1,), (1,)), ((), ()))

f32 = jnp.float32
bf16 = jnp.bfloat16


def _split(x):
    hi = x.astype(bf16)
    lo = (x - hi.astype(f32)).astype(bf16)
    return hi, lo


def _dg(a, b, dims=_NN):
    return lax.dot_general(a, b, dims, preferred_element_type=f32)


def _dot3(a, b, dims=_NN):
    return _dg(a[0], b[0], dims) + (_dg(a[0], b[1], dims) + _dg(a[1], b[0], dims))


def _rms(x):
    return x * lax.rsqrt(jnp.mean(x * x, axis=-1, keepdims=True) + EPS)


def _mod_kernel(c_ref, w_ref, b_ref, o_ref):
    c = c_ref[...]
    ca = c * (1.0 / (1.0 + jnp.exp(-c)))
    o_ref[...] = _dot3(_split(ca), _split(w_ref[...])) + b_ref[...]


def _mod_call(c_pad, ada_w, ada_b):
    nb = 1536
    n = ada_w.shape[1]
    return pl.pallas_call(
        _mod_kernel,
        grid=(n // nb,),
        in_specs=[
            pl.BlockSpec((SUBLANES, D_MODEL), lambda j: (0, 0)),
            pl.BlockSpec((D_MODEL, nb), lambda j: (0, j)),
            pl.BlockSpec((1, nb), lambda j: (0, j)),
        ],
        out_specs=pl.BlockSpec((SUBLANES, nb), lambda j: (0, j)),
        out_shape=jax.ShapeDtypeStruct((SUBLANES, n), f32),
        compiler_params=pltpu.CompilerParams(
            dimension_semantics=("arbitrary",), vmem_limit_bytes=VMEM_LIMIT),
        name="mod",
    )(c_pad, ada_w, ada_b)


def _mixer_kernel(sink_ref, x_ref, pos_ref, mod_ref, g1_ref, win_ref, gq_ref, gk_ref,
                  freq_ref, sgn_ref, msame_ref, convw_ref, ga_ref, gc_ref, wout_ref,
                  g2_ref, x1_ref, h2t_ref, qbuf, kbuf, vbuf, abuf, ucarry):
    ts = x_ref.shape[0]
    s_idx = pl.program_id(1)
    x = x_ref[...]
    mod = mod_ref[...]
    shift1, scale1, gate1 = mod[0:1], mod[1:2], mod[2:3]
    shift2, scale2 = mod[3:4], mod[4:5]

    h = _rms(x) * g1_ref[...]
    hb = (h * (1.0 + scale1) + shift1).astype(bf16)

    def proj(c0, c1):
        return _dg(hb, win_ref[:, c0:c1])

    ang = pos_ref[...] * freq_ref[...]
    cosf = jnp.cos(ang)
    sins = jnp.sin(ang) * sgn_ref[...]
    msame = msame_ref[...]

    def headnorm_rope(t, g):
        sq = _split(t * t)
        ss = _dg(sq[0], msame) + _dg(sq[1], msame)
        tn = t * lax.rsqrt(ss * (1.0 / HEAD_DIM) + EPS) * g
        return tn * cosf + pltpu.roll(tn, LANES // 2, 1) * sins

    @pl.when(s_idx == 0)
    def _():
        kbuf[0:WINDOW, :] = jnp.zeros((WINDOW, 2 * LANES), f32)
        vbuf[0:WINDOW, :] = jnp.zeros((WINDOW, 2 * LANES), f32)
        ucarry[...] = jnp.zeros(ucarry.shape, f32)

    for c in range(ATTN_WIDTH // LANES):
        sl = slice(c * LANES, (c + 1) * LANES)
        qbuf[:, sl] = headnorm_rope(proj(Q_OFF + c * LANES, Q_OFF + (c + 1) * LANES), gq_ref[:, sl])
    for g in range(N_KV_HEADS):
        sl = slice(g * LANES, (g + 1) * LANES)
        kbuf[WINDOW:WINDOW + ts, sl] = headnorm_rope(
            proj(K_OFF + g * LANES, K_OFF + (g + 1) * LANES), gk_ref[:, sl])
    vbuf[WINDOW:WINDOW + ts, :] = proj(V_OFF, V_OFF + 2 * LANES)

    qi = lax.broadcasted_iota(jnp.int32, (WINDOW, 2 * WINDOW), 0)
    kj = lax.broadcasted_iota(jnp.int32, (WINDOW, 2 * WINDOW), 1)
    band = (kj > qi) & (kj <= qi + WINDOW)
    lane = lax.broadcasted_iota(jnp.int32, (1, LANES), 1)

    def attn_block(n, carry):
        r0 = pl.multiple_of(n * WINDOW, WINDOW)
        qb = qbuf[pl.ds(r0, WINDOW), :]
        kb = kbuf[pl.ds(r0, 2 * WINDOW), :]
        vb = vbuf[pl.ds(r0, 2 * WINDOW), :]
        first = jnp.logical_and(s_idx == 0, n == 0)
        valid = band & (kj >= jnp.where(first, WINDOW, 0))
        q_per_kv = N_Q_HEADS // N_KV_HEADS
        for g in range(N_KV_HEADS):
            kt = kb[:, g * LANES:(g + 1) * LANES].astype(bf16)
            vt = vb[:, g * LANES:(g + 1) * LANES].astype(bf16)
            qs = []
            for c in (2 * g, 2 * g + 1):
                qt = qb[:, c * LANES:(c + 1) * LANES]
                for sub in range(2):
                    qs.append(jnp.where((lane // HALF_DIM) % 2 == sub, qt, 0.0).astype(bf16))
            s_all = _dg(jnp.concatenate(qs, axis=0), kt, _NT) * (HEAD_DIM ** -0.5)
            ps = []
            for hh in range(q_per_kv):
                s = jnp.where(valid, s_all[hh * WINDOW:(hh + 1) * WINDOW, :], NEG)
                sink = sink_ref[q_per_kv * g + hh]
                m = jnp.maximum(jnp.max(s, axis=-1, keepdims=True), sink)
                e = jnp.exp(s - m)
                den = jnp.sum(e, axis=-1, keepdims=True) + jnp.exp(sink - m)
                ps.append((e / den).astype(bf16))
            o_all = _dg(jnp.concatenate(ps, axis=0), vt)
            for cc in range(2):
                o = jnp.where(lane < HEAD_DIM,
                              o_all[2 * cc * WINDOW:(2 * cc + 1) * WINDOW, :],
                              o_all[(2 * cc + 1) * WINDOW:(2 * cc + 2) * WINDOW, :])
                c = 2 * g + cc
                abuf[pl.ds(r0, WINDOW), c * LANES:(c + 1) * LANES] = o
        return carry

    lax.fori_loop(0, ts // WINDOW, attn_block, 0)
    kbuf[0:WINDOW, :] = kbuf[ts:ts + WINDOW, :]
    vbuf[0:WINDOW, :] = vbuf[ts:ts + WINDOW, :]

    cb = proj(CB_OFF, CB_OFF + CONV_WIDTH)
    u = proj(CC_OFF, CC_OFF + CONV_WIDTH) * proj(CU_OFF, CU_OFF + CONV_WIDTH)
    rowi = lax.broadcasted_iota(jnp.int32, u.shape, 0)
    prev1 = ucarry[SUBLANES - 1:SUBLANES, :]
    prev2 = ucarry[SUBLANES - 2:SUBLANES - 1, :]
    u1 = jnp.where(rowi == 0, prev1, pltpu.roll(u, 1, 0))
    u2 = jnp.where(rowi == 0, prev2, jnp.where(rowi == 1, prev1, pltpu.roll(u, 2, 0)))
    ucarry[...] = u[ts - SUBLANES:ts, :]
    w = convw_ref[...]
    conv = cb * (w[0:1] * u2 + w[1:2] * u1 + w[2:3] * u)

    ra = (_rms(abuf[...]) * ga_ref[...]).astype(bf16)
    rc = (_rms(conv) * gc_ref[...]).astype(bf16)
    y = _dg(ra, wout_ref[0:ATTN_WIDTH, :]) + _dg(rc, wout_ref[ATTN_WIDTH:, :])
    x1 = x + gate1 * y
    x1_ref[...] = x1
    h2 = _rms(x1) * g2_ref[...]
    h2t_ref[...] = (h2 * (1.0 + scale2) + shift2).T.astype(bf16)


def _const_spec(shape):
    nd = len(shape)
    return pl.BlockSpec(shape, lambda *_: (0,) * nd, pipeline_mode=pl.Buffered(1))


def _mixer_call(sinks, x, pos, mod, g1, win, gq, gk, freq, sgn, msame, convw, ga, gc, wout, g2):
    B, S, D = x.shape
    ts = SEQ_TILE
    nst = S // ts
    return pl.pallas_call(
        _mixer_kernel,
        grid=(B, nst),
        in_specs=[
            pl.BlockSpec(memory_space=pltpu.SMEM),
            pl.BlockSpec((None, ts, D), lambda b, s: (b, s, 0)),
            pl.BlockSpec((None, ts, 1), lambda b, s: (b, s, 0)),
            pl.BlockSpec((None, N_MOD, D), lambda b, s: (b, 0, 0)),
            _const_spec(g1.shape), _const_spec(win.shape),
            _const_spec(gq.shape), _const_spec(gk.shape), _const_spec(freq.shape),
            _const_spec(sgn.shape), _const_spec(msame.shape), _const_spec(convw.shape),
            _const_spec(ga.shape), _const_spec(gc.shape), _const_spec(wout.shape),
            _const_spec(g2.shape),
        ],
        out_specs=[
            pl.BlockSpec((None, ts, D), lambda b, s: (b, s, 0)),
            pl.BlockSpec((D, ts), lambda b, s: (0, b * nst + s)),
        ],
        out_shape=[
            jax.ShapeDtypeStruct((B, S, D), f32),
            jax.ShapeDtypeStruct((D, B * S), bf16),
        ],
        scratch_shapes=[
            pltpu.VMEM((ts, ATTN_WIDTH), f32),
            pltpu.VMEM((WINDOW + ts, 2 * LANES), f32),
            pltpu.VMEM((WINDOW + ts, 2 * LANES), f32),
            pltpu.VMEM((ts, ATTN_WIDTH), f32),
            pltpu.VMEM((SUBLANES, CONV_WIDTH), f32),
        ],
        compiler_params=pltpu.CompilerParams(
            dimension_semantics=("arbitrary", "arbitrary"), vmem_limit_bytes=VMEM_LIMIT),
        name="mixer",
    )(sinks, x, pos, mod, g1, win, gq, gk, freq, sgn, msame, convw, ga, gc, wout, g2)


def _oddeven_merge_sort_pairs(n):
    pairs = []
    p = 1
    while p < n:
        k = p
        while k >= 1:
            for j in range(k % p, n - k, 2 * k):
                for i in range(min(k, n - j - k)):
                    if (i + j) // (2 * p) == (i + j + k) // (2 * p):
                        pairs.append((i + j, i + j + k))
            k //= 2
        p *= 2
    return pairs


_SORT16 = _oddeven_merge_sort_pairs(PEER_TOPK)


def _bitonic_desc(z):
    z = list(z)
    d = PEER_TOPK // 2
    while d >= 1:
        for r in range(PEER_TOPK):
            if not r & d:
                hi, lo = jnp.maximum(z[r], z[r + d]), jnp.minimum(z[r], z[r + d])
                z[r], z[r + d] = hi, lo
        d //= 2
    return z


def _merge_top(R, L):
    z = list(R)
    for r in range(PEER_TOPK - len(L), PEER_TOPK):
        z[r] = jnp.maximum(R[r], L[PEER_TOPK - 1 - r])
    return _bitonic_desc(z)


def _top16_rows(sc):
    x = [sc[SUBLANES * g:SUBLANES * (g + 1), :] for g in range(PEER_KEYS // SUBLANES)]
    for i, j in _SORT16:
        x[i], x[j] = jnp.maximum(x[i], x[j]), jnp.minimum(x[i], x[j])
    for shift in (4, 2, 1):
        z = [jnp.maximum(x[r], pltpu.roll(x[PEER_TOPK - 1 - r], shift, 0)) for r in range(PEER_TOPK)]
        x = _bitonic_desc(z)
    return x


def _router_kernel(h2t_ref, wq_ref, sk_ref, rows_ref, gate_ref, q_s, sc_s, top_s, row_s):
    tr = h2t_ref.shape[1]
    q_s[...] = _dg(wq_ref[...], h2t_ref[...])
    top_s[...] = jnp.zeros(top_s.shape, f32)
    sub = lax.broadcasted_iota(jnp.int32, (SUBLANES, tr), 0)

    def head_body(h, carry):
        for p in range(2):
            hp = 2 * h + p
            r0 = pl.multiple_of(hp * PEER_KEYS, PEER_KEYS)
            sc = _dg(sk_ref[hp], q_s[pl.ds(r0, PEER_KEYS), :].astype(bf16))
            sc_s[hp] = sc
            top = _top16_rows(sc)
            for r in range(PEER_TOPK):
                top_s[p, r] = jnp.where(sub == h, top[r], top_s[p, r])
        return carry

    lax.fori_loop(0, PEER_HEADS, head_body, 0)

    a = [top_s[0, r] for r in range(PEER_TOPK)]
    b = [top_s[1, r] for r in range(PEER_TOPK)]
    R = [a[0] + b[r] for r in range(16)]
    R = _merge_top(R, [a[r] + b[0] for r in range(1, 16)])
    R = _merge_top(R, [a[1] + b[r] for r in range(1, 8)])
    R = _merge_top(R, [a[r] + b[1] for r in range(2, 8)])
    R = _merge_top(R, [a[2] + b[r] for r in range(2, 5)])
    R = _merge_top(R, [a[r] + b[2] for r in range(3, 5)])
    R = _merge_top(R, [a[3] + b[3]])
    tau = R[PEER_TOPK - 1]
    z = jnp.exp(R[0] - R[0])
    for r in range(1, PEER_TOPK):
        z = z + jnp.exp(R[r] - R[0])
    row_s[0] = a[0]
    row_s[1] = b[0]
    row_s[2] = 0.5 / z
    row_s[3] = tau

    def fac_body(h, carry):
        row = pl.ds(h, 1)
        s1 = sc_s[2 * h]
        s2 = sc_s[2 * h + 1]
        tau_h = row_s[3, row, :]
        cnt = jnp.zeros(s1.shape, f32)
        rk = jnp.zeros(s2.shape, f32)
        for r in range(PEER_TOPK):
            b_r = top_s[1, r, row, :]
            cnt = jnp.where(s1 + b_r >= tau_h, float(r + 1), cnt)
            rk = jnp.where(b_r > s2, float(r + 1), rk)
        f1 = jnp.exp(s1 - row_s[0, row, :]) * row_s[2, row, :]
        f2 = jnp.exp(s2 - row_s[1, row, :])
        for tc in range(tr // LANES):
            cols = slice(tc * LANES, (tc + 1) * LANES)
            for ig in range(N_ROW_GROUPS):
                keys = slice(ig * SUBLANES, (ig + 1) * SUBLANES)
                rows_ref[tc, ig, h, 0] = f1[keys, cols]
                rows_ref[tc, ig, h, 1] = cnt[keys, cols]
            for v in range(N_KEY_VREGS):
                keys = slice(v * BF16_ROWS, (v + 1) * BF16_ROWS)
                gate_ref[tc, v, h, 0] = rk[keys, cols]
                gate_ref[tc, v, h, 1] = f2[keys, cols]
        return carry

    lax.fori_loop(0, PEER_HEADS, fac_body, 0)


def _router_call(h2t, wq, sk):
    D, T = h2t.shape
    tr = ROUTER_TILE
    rows_shape = (N_ROW_GROUPS, PEER_HEADS, 2, SUBLANES, LANES)
    gate_shape = (N_KEY_VREGS, PEER_HEADS, 2, BF16_ROWS, LANES)
    lead = lambda i: (i, 0, 0, 0, 0, 0)
    return pl.pallas_call(
        _router_kernel,
        grid=(T // tr,),
        in_specs=[
            pl.BlockSpec((D, tr), lambda i: (0, i)),
            _const_spec(wq.shape), _const_spec(sk.shape),
        ],
        out_specs=[
            pl.BlockSpec((tr // LANES,) + rows_shape, lead),
            pl.BlockSpec((tr // LANES,) + gate_shape, lead),
        ],
        out_shape=[
            jax.ShapeDtypeStruct((T // LANES,) + rows_shape, f32),
            jax.ShapeDtypeStruct((T // LANES,) + gate_shape, f32),
        ],
        scratch_shapes=[
            pltpu.VMEM((PEER_HEADS * PEER_QDIM, tr), f32),
            pltpu.VMEM((2 * PEER_HEADS, PEER_KEYS, tr), f32),
            pltpu.VMEM((2, PEER_TOPK, SUBLANES, tr), f32),
            pltpu.VMEM((4, SUBLANES, tr), f32),
        ],
        compiler_params=pltpu.CompilerParams(
            dimension_semantics=("arbitrary",), vmem_limit_bytes=VMEM_LIMIT),
        name="router",
    )(h2t, wq, sk)


_GELU_C1 = float(np.sqrt(2.0 / np.pi))
_GELU_C2 = 0.044715 * _GELU_C1


def _experts_kernel(h2b_ref, rows_ref, gate_ref, u_ref, vt_ref, x1_ref, mod_ref,
                    out_ref, acc, a_s, w_s, gate_s):
    tm = h2b_ref.shape[1]
    te = u_ref.shape[0]
    assert te == SUBLANES * PEER_KEYS
    j = pl.program_id(1)

    @pl.when(j == 0)
    def _():
        acc[...] = jnp.zeros(acc.shape, f32)
        gate_s[...] = gate_ref[...].astype(bf16)

    a_s[:, 0:tm] = _dg(u_ref[...], h2b_ref[...])

    def packed_row(tc, h, kind, r):
        row = rows_ref[tc, j, h, kind][r:r + 1, :]
        return jnp.broadcast_to(row, (BF16_ROWS, LANES)).astype(bf16)

    for r in range(SUBLANES):
        for tc in range(tm // LANES):
            cols = slice(tc * LANES, (tc + 1) * LANES)
            f1 = [packed_row(tc, h, 0, r) for h in range(PEER_HEADS)]
            cnt = [packed_row(tc, h, 1, r) for h in range(PEER_HEADS)]
            for v in range(N_KEY_VREGS):
                rows = slice(r * PEER_KEYS + v * BF16_ROWS, r * PEER_KEYS + (v + 1) * BF16_ROWS)
                g = jnp.zeros((BF16_ROWS, LANES), bf16)
                for h in range(PEER_HEADS):
                    sel = jnp.where(gate_s[tc, v, h, 0] < cnt[h], gate_s[tc, v, h, 1],
                                    jnp.zeros((), bf16))
                    g = g + sel * f1[h]
                a = a_s[rows, cols]
                t = jnp.tanh(a * (_GELU_C1 + _GELU_C2 * (a * a)))
                w_s[rows, cols] = g * (a * (1.0 + t)).astype(bf16)

    acc[...] += _dg(vt_ref[...], w_s[:, 0:tm])

    @pl.when(j == pl.num_programs(1) - 1)
    def _():
        gate2 = mod_ref[N_MOD - 1:N_MOD, :]
        out_ref[...] = x1_ref[...] + gate2 * acc[...].T


def _experts_call(h2b, rows, gate, u_b, vt_b, x1, mod, tiles_per_batch):
    D, T = h2b.shape
    tm, te = TOK_TILE, EXP_TILE
    lead = lambda i, j: (i, 0, 0, 0, 0, 0)
    return pl.pallas_call(
        _experts_kernel,
        grid=(T // tm, PEER_EXPERTS // te),
        in_specs=[
            pl.BlockSpec((D, tm), lambda i, j: (0, i)),
            pl.BlockSpec((tm // LANES,) + rows.shape[1:], lead),
            pl.BlockSpec((tm // LANES,) + gate.shape[1:], lead),
            pl.BlockSpec((te, D), lambda i, j: (j, 0)),
            pl.BlockSpec((D, te), lambda i, j: (0, j)),
            pl.BlockSpec((tm, D), lambda i, j: (i, 0)),
            pl.BlockSpec((None, N_MOD, D), lambda i, j: (i // tiles_per_batch, 0, 0)),
        ],
        out_specs=pl.BlockSpec((tm, D), lambda i, j: (i, 0)),
        out_shape=jax.ShapeDtypeStruct((T, D), f32),
        scratch_shapes=[
            pltpu.VMEM((D, tm), f32),
            pltpu.VMEM((te, tm + LANES), f32),
            pltpu.VMEM((te, tm + LANES), bf16),
            pltpu.VMEM((tm // LANES,) + gate.shape[1:], bf16),
        ],
        compiler_params=pltpu.CompilerParams(
            dimension_semantics=("arbitrary", "arbitrary"), vmem_limit_bytes=VMEM_LIMIT),
        name="experts",
    )(h2b, rows, gate, u_b, vt_b, x1, mod)


def _proj_columns():
    half = np.arange(HALF_DIM)
    cols = []
    for c in range(N_Q_HEADS // 2):
        for part in range(2):
            for head in (2 * c, 2 * c + 1):
                cols.append(head * HEAD_DIM + part * HALF_DIM + half)
    for g in range(N_KV_HEADS):
        for part in (0, 0, 1, 1):
            cols.append(ATTN_WIDTH + g * HEAD_DIM + part * HALF_DIM + half)
    for g in range(N_KV_HEADS):
        for _ in range(2):
            cols.append(ATTN_WIDTH + KV_WIDTH + g * HEAD_DIM + np.arange(HEAD_DIM))
    cols.append(np.arange(ATTN_WIDTH + 2 * KV_WIDTH, ATTN_WIDTH + 2 * KV_WIDTH + 3 * CONV_WIDTH))
    return np.concatenate(cols)


_PROJ_COLS = _proj_columns()
_TILE_DIM = np.concatenate([np.arange(HALF_DIM), np.arange(HALF_DIM),
                            HALF_DIM + np.arange(HALF_DIM), HALF_DIM + np.arange(HALF_DIM)])
_TILE_HEAD = (np.arange(LANES) // HALF_DIM) % 2
_MSAME = (_TILE_HEAD[:, None] == _TILE_HEAD[None, :]).astype(np.float32)


def kernel(x, c, positions, ada_w, ada_b, norm1_g, w_in, q_norm_g, k_norm_g, sinks, conv_w,
           attn_out_g, conv_out_g, w_out, norm2_g, peer_wq, peer_subkeys, peer_u, peer_v):
    B, S, D = x.shape
    T = B * S
    assert ada_w.shape[0] == 1, "single-layer block"
    assert D == D_MODEL and S % SEQ_TILE == 0 and T % TOK_TILE == 0 and T % ROUTER_TILE == 0
    assert S % TOK_TILE == 0

    c_pad = jnp.pad(c, ((0, SUBLANES - B), (0, 0)))
    mod = _mod_call(c_pad, ada_w[0], ada_b[0][None, :])[:B].reshape(B, N_MOD, D)

    inv_freq = ROPE_THETA ** (-jnp.arange(0, HEAD_DIM, 2, dtype=f32) / HEAD_DIM)
    freq = inv_freq[_TILE_DIM % HALF_DIM][None, :]
    sgn = jnp.asarray(np.where(np.arange(LANES) < LANES // 2, -1.0, 1.0), f32)[None, :]
    win = w_in[0][:, _PROJ_COLS].astype(bf16)
    wout = w_out[0].astype(bf16)
    gq = jnp.tile(q_norm_g[0][_TILE_DIM], ATTN_WIDTH // LANES)[None, :]
    gk = jnp.tile(k_norm_g[0][_TILE_DIM], N_KV_HEADS)[None, :]
    x1, h2t = _mixer_call(
        sinks[0], x, positions.astype(f32)[..., None], mod, norm1_g, win, gq, gk, freq, sgn,
        jnp.asarray(_MSAME, bf16), conv_w[0], attn_out_g, conv_out_g, wout, norm2_g)

    wq = peer_wq[0].T.astype(bf16)
    sk = peer_subkeys[0].reshape(2 * PEER_HEADS, PEER_KEYS, PEER_QDIM // 2).astype(bf16)
    rows, gate = _router_call(h2t, wq, sk)

    out = _experts_call(h2t, rows, gate, peer_u[0].astype(bf16), peer_v[0].T.astype(bf16),
                        x1.reshape(T, D), mod, S // TOK_TILE)
    return out.reshape(B, S, D)
```

```python
import numpy as np
import jax
import jax.numpy as jnp
from jax import lax
from jax.experimental import pallas as pl
from jax.experimental.pallas import tpu as pltpu

D_MODEL = 1024
HEAD_DIM = 64
HALF_DIM = HEAD_DIM // 2
N_Q_HEADS = 8
N_KV_HEADS = 2
ATTN_WIDTH = N_Q_HEADS * HEAD_DIM
KV_WIDTH = N_KV_HEADS * HEAD_DIM
WINDOW = 128
ROPE_THETA = 10000.0
CONV_WIDTH = D_MODEL - ATTN_WIDTH
CONV_K = 3
PEER_HEADS = 8
PEER_KEYS = 128
PEER_EXPERTS = PEER_KEYS * PEER_KEYS
PEER_TOPK = 16
PEER_QDIM = 256
N_MOD = 6
EPS = 1e-6
NEG = -1e30

LANES = 128
SUBLANES = 8
BF16_ROWS = 2 * SUBLANES
N_ROW_GROUPS = PEER_KEYS // SUBLANES
N_KEY_VREGS = PEER_KEYS // BF16_ROWS
VMEM_LIMIT = 56 * 1024 * 1024

Q_OFF = 0
K_OFF = ATTN_WIDTH
V_OFF = K_OFF + 2 * KV_WIDTH
CB_OFF = V_OFF + 2 * KV_WIDTH
CC_OFF = CB_OFF + CONV_WIDTH
CU_OFF = CC_OFF + CONV_WIDTH
PROJ_WIDTH = CU_OFF + CONV_WIDTH

SEQ_TILE = 512
ROUTER_TILE = 512
TOK_TILE = 512
EXP_TILE = 1024
MXU_SLICES = 4

_NN = (((1,), (0,)), ((), ()))
_NT = (((1,), (1,)), ((), ()))

f32 = jnp.float32
bf16 = jnp.bfloat16


def _split(x):
    hi = x.astype(bf16)
    lo = (x - hi.astype(f32)).astype(bf16)
    return hi, lo


def _dg(a, b, dims=_NN):
    return lax.dot_general(a, b, dims, preferred_element_type=f32)


def _dot3(a, b, dims=_NN):
    return _dg(a[0], b[0], dims) + (_dg(a[0], b[1], dims) + _dg(a[1], b[0], dims))


def _rms(x):
    return x * lax.rsqrt(jnp.mean(x * x, axis=-1, keepdims=True) + EPS)


def _mod_kernel(c_ref, w_ref, b_ref, o_ref):
    c = c_ref[...]
    ca = c * (1.0 / (1.0 + jnp.exp(-c)))
    o_ref[...] = _dot3(_split(ca), _split(w_ref[...])) + b_ref[...]


def _mod_call(c_pad, ada_w, ada_b):
    nb = 1536
    n = ada_w.shape[1]
    return pl.pallas_call(
        _mod_kernel,
        grid=(n // nb,),
        in_specs=[
            pl.BlockSpec((SUBLANES, D_MODEL), lambda j: (0, 0)),
            pl.BlockSpec((D_MODEL, nb), lambda j: (0, j)),
            pl.BlockSpec((1, nb), lambda j: (0, j)),
        ],
        out_specs=pl.BlockSpec((SUBLANES, nb), lambda j: (0, j)),
        out_shape=jax.ShapeDtypeStruct((SUBLANES, n), f32),
        compiler_params=pltpu.CompilerParams(
            dimension_semantics=("arbitrary",), vmem_limit_bytes=VMEM_LIMIT),
        name="mod",
    )(c_pad, ada_w, ada_b)


def _mixer_kernel(sink_ref, x_ref, pos_ref, mod_ref, g1_ref, win_ref, gq_ref, gk_ref,
                  freq_ref, sgn_ref, msame_ref, convw_ref, ga_ref, gc_ref, wout_ref,
                  g2_ref, x1_ref, h2t_ref, qbuf, kbuf, vbuf, abuf, ucarry):
    ts = x_ref.shape[0]
    s_idx = pl.program_id(1)
    x = x_ref[...]
    mod = mod_ref[...]
    shift1, scale1, gate1 = mod[0:1], mod[1:2], mod[2:3]
    shift2, scale2 = mod[3:4], mod[4:5]

    h = _rms(x) * g1_ref[...]
    hb = (h * (1.0 + scale1) + shift1).astype(bf16)

    def proj(c0, c1):
        return _dg(hb, win_ref[:, c0:c1])

    ang = pos_ref[...] * freq_ref[...]
    cosf = jnp.cos(ang)
    sins = jnp.sin(ang) * sgn_ref[...]
    msame = msame_ref[...]

    def headnorm_rope(t, g):
        sq = _split(t * t)
        ss = _dg(sq[0], msame) + _dg(sq[1], msame)
        tn = t * lax.rsqrt(ss * (1.0 / HEAD_DIM) + EPS) * g
        return tn * cosf + pltpu.roll(tn, LANES // 2, 1) * sins

    @pl.when(s_idx == 0)
    def _():
        kbuf[0:WINDOW, :] = jnp.zeros((WINDOW, 2 * LANES), f32)
        vbuf[0:WINDOW, :] = jnp.zeros((WINDOW, 2 * LANES), f32)
        ucarry[...] = jnp.zeros(ucarry.shape, f32)

    for c in range(ATTN_WIDTH // LANES):
        sl = slice(c * LANES, (c + 1) * LANES)
        qbuf[:, sl] = headnorm_rope(proj(Q_OFF + c * LANES, Q_OFF + (c + 1) * LANES), gq_ref[:, sl])
    for g in range(N_KV_HEADS):
        sl = slice(g * LANES, (g + 1) * LANES)
        kbuf[WINDOW:WINDOW + ts, sl] = headnorm_rope(
            proj(K_OFF + g * LANES, K_OFF + (g + 1) * LANES), gk_ref[:, sl])
    vbuf[WINDOW:WINDOW + ts, :] = proj(V_OFF, V_OFF + 2 * LANES)

    qi = lax.broadcasted_iota(jnp.int32, (WINDOW, 2 * WINDOW), 0)
    kj = lax.broadcasted_iota(jnp.int32, (WINDOW, 2 * WINDOW), 1)
    band = (kj > qi) & (kj <= qi + WINDOW)
    lane = lax.broadcasted_iota(jnp.int32, (1, LANES), 1)

    def attn_block(n, carry):
        r0 = pl.multiple_of(n * WINDOW, WINDOW)
        qb = qbuf[pl.ds(r0, WINDOW), :]
        kb = kbuf[pl.ds(r0, 2 * WINDOW), :]
        vb = vbuf[pl.ds(r0, 2 * WINDOW), :]
        first = jnp.logical_and(s_idx == 0, n == 0)
        valid = band & (kj >= jnp.where(first, WINDOW, 0))
        q_per_kv = N_Q_HEADS // N_KV_HEADS
        for g in range(N_KV_HEADS):
            kt = kb[:, g * LANES:(g + 1) * LANES].astype(bf16)
            vt = vb[:, g * LANES:(g + 1) * LANES].astype(bf16)
            qs = []
            for c in (2 * g, 2 * g + 1):
                qt = qb[:, c * LANES:(c + 1) * LANES]
                for sub in range(2):
                    qs.append(jnp.where((lane // HALF_DIM) % 2 == sub, qt, 0.0).astype(bf16))
            s_all = _dg(jnp.concatenate(qs, axis=0), kt, _NT) * (HEAD_DIM ** -0.5)
            ps = []
            for hh in range(q_per_kv):
                s = jnp.where(valid, s_all[hh * WINDOW:(hh + 1) * WINDOW, :], NEG)
                sink = sink_ref[q_per_kv * g + hh]
                m = jnp.maximum(jnp.max(s, axis=-1, keepdims=True), sink)
                e = jnp.exp(s - m)
                den = jnp.sum(e, axis=-1, keepdims=True) + jnp.exp(sink - m)
                ps.append((e / den).astype(bf16))
            o_all = _dg(jnp.concatenate(ps, axis=0), vt)
            for cc in range(2):
                o = jnp.where(lane < HEAD_DIM,
                              o_all[2 * cc * WINDOW:(2 * cc + 1) * WINDOW, :],
                              o_all[(2 * cc + 1) * WINDOW:(2 * cc + 2) * WINDOW, :])
                c = 2 * g + cc
                abuf[pl.ds(r0, WINDOW), c * LANES:(c + 1) * LANES] = o
        return carry

    lax.fori_loop(0, ts // WINDOW, attn_block, 0)
    kbuf[0:WINDOW, :] = kbuf[ts:ts + WINDOW, :]
    vbuf[0:WINDOW, :] = vbuf[ts:ts + WINDOW, :]

    cb = proj(CB_OFF, CB_OFF + CONV_WIDTH)
    u = proj(CC_OFF, CC_OFF + CONV_WIDTH) * proj(CU_OFF, CU_OFF + CONV_WIDTH)
    rowi = lax.broadcasted_iota(jnp.int32, u.shape, 0)
    prev1 = ucarry[SUBLANES - 1:SUBLANES, :]
    prev2 = ucarry[SUBLANES - 2:SUBLANES - 1, :]
    u1 = jnp.where(rowi == 0, prev1, pltpu.roll(u, 1, 0))
    u2 = jnp.where(rowi == 0, prev2, jnp.where(rowi == 1, prev1, pltpu.roll(u, 2, 0)))
    ucarry[...] = u[ts - SUBLANES:ts, :]
    w = convw_ref[...]
    conv = cb * (w[0:1] * u2 + w[1:2] * u1 + w[2:3] * u)

    ra = (_rms(abuf[...]) * ga_ref[...]).astype(bf16)
    rc = (_rms(conv) * gc_ref[...]).astype(bf16)
    y = _dg(ra, wout_ref[0:ATTN_WIDTH, :]) + _dg(rc, wout_ref[ATTN_WIDTH:, :])
    x1 = x + gate1 * y
    x1_ref[...] = x1
    h2 = _rms(x1) * g2_ref[...]
    h2t_ref[...] = (h2 * (1.0 + scale2) + shift2).T.astype(bf16)


def _const_spec(shape):
    nd = len(shape)
    return pl.BlockSpec(shape, lambda *_: (0,) * nd, pipeline_mode=pl.Buffered(1))


def _mixer_call(sinks, x, pos, mod, g1, win, gq, gk, freq, sgn, msame, convw, ga, gc, wout, g2):
    B, S, D = x.shape
    ts = SEQ_TILE
    nst = S // ts
    return pl.pallas_call(
        _mixer_kernel,
        grid=(B, nst),
        in_specs=[
            pl.BlockSpec(memory_space=pltpu.SMEM),
            pl.BlockSpec((None, ts, D), lambda b, s: (b, s, 0)),
            pl.BlockSpec((None, ts, 1), lambda b, s: (b, s, 0)),
            pl.BlockSpec((None, N_MOD, D), lambda b, s: (b, 0, 0)),
            _const_spec(g1.shape), _const_spec(win.shape),
            _const_spec(gq.shape), _const_spec(gk.shape), _const_spec(freq.shape),
            _const_spec(sgn.shape), _const_spec(msame.shape), _const_spec(convw.shape),
            _const_spec(ga.shape), _const_spec(gc.shape), _const_spec(wout.shape),
            _const_spec(g2.shape),
        ],
        out_specs=[
            pl.BlockSpec((None, ts, D), lambda b, s: (b, s, 0)),
            pl.BlockSpec((D, ts), lambda b, s: (0, b * nst + s)),
        ],
        out_shape=[
            jax.ShapeDtypeStruct((B, S, D), f32),
            jax.ShapeDtypeStruct((D, B * S), bf16),
        ],
        scratch_shapes=[
            pltpu.VMEM((ts, ATTN_WIDTH), f32),
            pltpu.VMEM((WINDOW + ts, 2 * LANES), f32),
            pltpu.VMEM((WINDOW + ts, 2 * LANES), f32),
            pltpu.VMEM((ts, ATTN_WIDTH), f32),
            pltpu.VMEM((SUBLANES, CONV_WIDTH), f32),
        ],
        compiler_params=pltpu.CompilerParams(
            dimension_semantics=("arbitrary", "arbitrary"), vmem_limit_bytes=VMEM_LIMIT),
        name="mixer",
    )(sinks, x, pos, mod, g1, win, gq, gk, freq, sgn, msame, convw, ga, gc, wout, g2)


def _oddeven_merge_sort_pairs(n):
    pairs = []
    p = 1
    while p < n:
        k = p
        while k >= 1:
            for j in range(k % p, n - k, 2 * k):
                for i in range(min(k, n - j - k)):
                    if (i + j) // (2 * p) == (i + j + k) // (2 * p):
                        pairs.append((i + j, i + j + k))
            k //= 2
        p *= 2
    return pairs


_SORT16 = _oddeven_merge_sort_pairs(PEER_TOPK)


def _bitonic_desc(z):
    z = list(z)
    d = PEER_TOPK // 2
    while d >= 1:
        for r in range(PEER_TOPK):
            if not r & d:
                hi, lo = jnp.maximum(z[r], z[r + d]), jnp.minimum(z[r], z[r + d])
                z[r], z[r + d] = hi, lo
        d //= 2
    return z


def _merge_top(R, L):
    z = list(R)
    for r in range(PEER_TOPK - len(L), PEER_TOPK):
        z[r] = jnp.maximum(R[r], L[PEER_TOPK - 1 - r])
    return _bitonic_desc(z)


def _top16_rows(sc):
    x = [sc[SUBLANES * g:SUBLANES * (g + 1), :] for g in range(PEER_KEYS // SUBLANES)]
    for i, j in _SORT16:
        x[i], x[j] = jnp.maximum(x[i], x[j]), jnp.minimum(x[i], x[j])
    for shift in (4, 2, 1):
        z = [jnp.maximum(x[r], pltpu.roll(x[PEER_TOPK - 1 - r], shift, 0)) for r in range(PEER_TOPK)]
        x = _bitonic_desc(z)
    return x


def _prefix_count(test, rows):
    assert len(rows) == PEER_TOPK == 16
    sel = jnp.where
    p3 = test(rows[7])
    p2 = test(sel(p3, rows[11], rows[3]))
    p1 = test(sel(p3, sel(p2, rows[13], rows[9]), sel(p2, rows[5], rows[1])))
    hi = sel(p2, sel(p1, rows[14], rows[12]), sel(p1, rows[10], rows[8]))
    lo = sel(p2, sel(p1, rows[6], rows[4]), sel(p1, rows[2], rows[0]))
    p0 = test(sel(p3, hi, lo))
    bits = ((p3, 8.0), (p2, 4.0), (p1, 2.0), (p0, 1.0), (test(rows[15]), 1.0))
    return sum(sel(p, v, 0.0) for p, v in bits)


def _router_kernel(h2t_ref, wq_ref, sk_ref, rows_ref, gate_ref, q_s, sc_s, top_s, row_s):
    tr = h2t_ref.shape[1]
    q_s[...] = _dg(wq_ref[...], h2t_ref[...])
    top_s[...] = jnp.zeros(top_s.shape, f32)
    sub = lax.broadcasted_iota(jnp.int32, (SUBLANES, tr), 0)

    def head_body(h, carry):
        for p in range(2):
            hp = 2 * h + p
            r0 = pl.multiple_of(hp * PEER_KEYS, PEER_KEYS)
            sc = _dg(sk_ref[hp], q_s[pl.ds(r0, PEER_KEYS), :].astype(bf16))
            sc_s[hp] = sc
            top = _top16_rows(sc)
            for r in range(PEER_TOPK):
                top_s[p, r] = jnp.where(sub == h, top[r], top_s[p, r])
        return carry

    lax.fori_loop(0, PEER_HEADS, head_body, 0)

    a = [top_s[0, r] for r in range(PEER_TOPK)]
    b = [top_s[1, r] for r in range(PEER_TOPK)]
    R = [a[0] + b[r] for r in range(16)]
    R = _merge_top(R, [a[r] + b[0] for r in range(1, 16)])
    R = _merge_top(R, [a[1] + b[r] for r in range(1, 8)])
    R = _merge_top(R, [a[r] + b[1] for r in range(2, 8)])
    R = _merge_top(R, [a[2] + b[r] for r in range(2, 5)])
    R = _merge_top(R, [a[r] + b[2] for r in range(3, 5)])
    R = _merge_top(R, [a[3] + b[3]])
    tau = R[PEER_TOPK - 1]
    z = jnp.exp(R[0] - R[0])
    for r in range(1, PEER_TOPK):
        z = z + jnp.exp(R[r] - R[0])
    row_s[0] = a[0]
    row_s[1] = b[0]
    row_s[2] = 0.5 / z
    row_s[3] = tau

    def fac_body(h, carry):
        row = pl.ds(h, 1)
        s1 = sc_s[2 * h]
        s2 = sc_s[2 * h + 1]
        tau_h = row_s[3, row, :]
        b_rows = [top_s[1, r, row, :] for r in range(PEER_TOPK)]
        cnt = _prefix_count(lambda b_r: s1 + b_r >= tau_h, b_rows)
        rk = _prefix_count(lambda b_r: b_r > s2, b_rows)
        f1 = jnp.exp(s1 - row_s[0, row, :]) * row_s[2, row, :]
        f2 = jnp.exp(s2 - row_s[1, row, :])
        for tc in range(tr // LANES):
            cols = slice(tc * LANES, (tc + 1) * LANES)
            for ig in range(N_ROW_GROUPS):
                keys = slice(ig * SUBLANES, (ig + 1) * SUBLANES)
                rows_ref[tc, ig, h, 0] = f1[keys, cols]
                rows_ref[tc, ig, h, 1] = cnt[keys, cols]
            for v in range(N_KEY_VREGS):
                keys = slice(v * BF16_ROWS, (v + 1) * BF16_ROWS)
                gate_ref[tc, v, h, 0] = rk[keys, cols]
                gate_ref[tc, v, h, 1] = f2[keys, cols]
        return carry

    lax.fori_loop(0, PEER_HEADS, fac_body, 0)


def _router_call(h2t, wq, sk):
    D, T = h2t.shape
    tr = ROUTER_TILE
    rows_shape = (N_ROW_GROUPS, PEER_HEADS, 2, SUBLANES, LANES)
    gate_shape = (N_KEY_VREGS, PEER_HEADS, 2, BF16_ROWS, LANES)
    lead = lambda i: (i, 0, 0, 0, 0, 0)
    return pl.pallas_call(
        _router_kernel,
        grid=(T // tr,),
        in_specs=[
            pl.BlockSpec((D, tr), lambda i: (0, i)),
            _const_spec(wq.shape), _const_spec(sk.shape),
        ],
        out_specs=[
            pl.BlockSpec((tr // LANES,) + rows_shape, lead),
            pl.BlockSpec((tr // LANES,) + gate_shape, lead),
        ],
        out_shape=[
            jax.ShapeDtypeStruct((T // LANES,) + rows_shape, f32),
            jax.ShapeDtypeStruct((T // LANES,) + gate_shape, f32),
        ],
        scratch_shapes=[
            pltpu.VMEM((PEER_HEADS * PEER_QDIM, tr), f32),
            pltpu.VMEM((2 * PEER_HEADS, PEER_KEYS, tr), f32),
            pltpu.VMEM((2, PEER_TOPK, SUBLANES, tr), f32),
            pltpu.VMEM((4, SUBLANES, tr), f32),
        ],
        compiler_params=pltpu.CompilerParams(
            dimension_semantics=("arbitrary",), vmem_limit_bytes=VMEM_LIMIT),
        name="router",
    )(h2t, wq, sk)


_GELU_C1 = float(np.sqrt(2.0 / np.pi))
_GELU_C2 = 0.044715 * _GELU_C1


def _experts_kernel(h2b_ref, rows_ref, gate_ref, u_ref, vt_ref, x1_ref, mod_ref,
                    out_ref, acc, a_s, w_s, gate_s):
    tm = h2b_ref.shape[1]
    te = u_ref.shape[0]
    assert te == SUBLANES * PEER_KEYS
    j = pl.program_id(1)

    @pl.when(j == 0)
    def _():
        acc[...] = jnp.zeros(acc.shape, f32)
        gate_s[...] = gate_ref[...].astype(bf16)

    for m in range(MXU_SLICES):
        rows = slice(m * (te // MXU_SLICES), (m + 1) * (te // MXU_SLICES))
        a = _dg(u_ref[rows, :], h2b_ref[...])
        t = jnp.tanh(a * (_GELU_C1 + _GELU_C2 * (a * a)))
        a_s[rows, 0:tm] = (a * (1.0 + t)).astype(bf16)

    def packed_row(tc, h, kind, r):
        row = rows_ref[tc, j, h, kind][r:r + 1, :]
        return jnp.broadcast_to(row, (BF16_ROWS, LANES)).astype(bf16)

    @pl.when(j < pl.num_programs(1))
    def _():
        for r in range(SUBLANES):
            for tc in range(tm // LANES):
                cols = slice(tc * LANES, (tc + 1) * LANES)
                f1 = [packed_row(tc, h, 0, r) for h in range(PEER_HEADS)]
                cnt = [packed_row(tc, h, 1, r) for h in range(PEER_HEADS)]
                for v in range(N_KEY_VREGS):
                    rows = slice(r * PEER_KEYS + v * BF16_ROWS,
                                 r * PEER_KEYS + (v + 1) * BF16_ROWS)
                    g = jnp.zeros((BF16_ROWS, LANES), bf16)
                    for h in range(PEER_HEADS):
                        sel = jnp.where(gate_s[tc, v, h, 0] < cnt[h], gate_s[tc, v, h, 1],
                                        jnp.zeros((), bf16))
                        g = g + sel * f1[h]
                    w_s[rows, cols] = g * a_s[rows, cols]

    for m in range(MXU_SLICES):
        rows = slice(m * (D_MODEL // MXU_SLICES), (m + 1) * (D_MODEL // MXU_SLICES))
        acc[rows, :] += _dg(vt_ref[rows, :], w_s[:, 0:tm])

    @pl.when(j == pl.num_programs(1) - 1)
    def _():
        gate2 = mod_ref[N_MOD - 1:N_MOD, :]
        out_ref[...] = x1_ref[...] + gate2 * acc[...].T


def _experts_call(h2b, rows, gate, u_b, vt_b, x1, mod, tiles_per_batch):
    D, T = h2b.shape
    tm, te = TOK_TILE, EXP_TILE
    lead = lambda i, j: (i, 0, 0, 0, 0, 0)
    return pl.pallas_call(
        _experts_kernel,
        grid=(T // tm, PEER_EXPERTS // te),
        in_specs=[
            pl.BlockSpec((D, tm), lambda i, j: (0, i)),
            pl.BlockSpec((tm // LANES,) + rows.shape[1:], lead),
            pl.BlockSpec((tm // LANES,) + gate.shape[1:], lead),
            pl.BlockSpec((te, D), lambda i, j: (j, 0)),
            pl.BlockSpec((D, te), lambda i, j: (0, j)),
            pl.BlockSpec((tm, D), lambda i, j: (i, 0)),
            pl.BlockSpec((None, N_MOD, D), lambda i, j: (i // tiles_per_batch, 0, 0)),
        ],
        out_specs=pl.BlockSpec((tm, D), lambda i, j: (i, 0)),
        out_shape=jax.ShapeDtypeStruct((T, D), f32),
        scratch_shapes=[
            pltpu.VMEM((D, tm), f32),
            pltpu.VMEM((te, tm + LANES), bf16),
            pltpu.VMEM((te, tm + LANES), bf16),
            pltpu.VMEM((tm // LANES,) + gate.shape[1:], bf16),
        ],
        compiler_params=pltpu.CompilerParams(
            dimension_semantics=("arbitrary", "arbitrary"), vmem_limit_bytes=VMEM_LIMIT),
        name="experts",
    )(h2b, rows, gate, u_b, vt_b, x1, mod)


def _proj_columns():
    half = np.arange(HALF_DIM)
    cols = []
    for c in range(N_Q_HEADS // 2):
        for part in range(2):
            for head in (2 * c, 2 * c + 1):
                cols.append(head * HEAD_DIM + part * HALF_DIM + half)
    for g in range(N_KV_HEADS):
        for part in (0, 0, 1, 1):
            cols.append(ATTN_WIDTH + g * HEAD_DIM + part * HALF_DIM + half)
    for g in range(N_KV_HEADS):
        for _ in range(2):
            cols.append(ATTN_WIDTH + KV_WIDTH + g * HEAD_DIM + np.arange(HEAD_DIM))
    cols.append(np.arange(ATTN_WIDTH + 2 * KV_WIDTH, ATTN_WIDTH + 2 * KV_WIDTH + 3 * CONV_WIDTH))
    return np.concatenate(cols)


_PROJ_COLS = _proj_columns()
_TILE_DIM = np.concatenate([np.arange(HALF_DIM), np.arange(HALF_DIM),
                            HALF_DIM + np.arange(HALF_DIM), HALF_DIM + np.arange(HALF_DIM)])
_TILE_HEAD = (np.arange(LANES) // HALF_DIM) % 2
_MSAME = (_TILE_HEAD[:, None] == _TILE_HEAD[None, :]).astype(np.float32)


def kernel(x, c, positions, ada_w, ada_b, norm1_g, w_in, q_norm_g, k_norm_g, sinks, conv_w,
           attn_out_g, conv_out_g, w_out, norm2_g, peer_wq, peer_subkeys, peer_u, peer_v):
    B, S, D = x.shape
    T = B * S
    assert ada_w.shape[0] == 1, "single-layer block"
    assert D == D_MODEL and S % SEQ_TILE == 0 and T % TOK_TILE == 0 and T % ROUTER_TILE == 0
    assert S % TOK_TILE == 0

    c_pad = jnp.pad(c, ((0, SUBLANES - B), (0, 0)))
    mod = _mod_call(c_pad, ada_w[0], ada_b[0][None, :])[:B].reshape(B, N_MOD, D)

    inv_freq = ROPE_THETA ** (-jnp.arange(0, HEAD_DIM, 2, dtype=f32) / HEAD_DIM)
    freq = inv_freq[_TILE_DIM % HALF_DIM][None, :]
    sgn = jnp.asarray(np.where(np.arange(LANES) < LANES // 2, -1.0, 1.0), f32)[None, :]
    win = w_in[0][:, _PROJ_COLS].astype(bf16)
    wout = w_out[0].astype(bf16)
    gq = jnp.tile(q_norm_g[0][_TILE_DIM], ATTN_WIDTH // LANES)[None, :]
    gk = jnp.tile(k_norm_g[0][_TILE_DIM], N_KV_HEADS)[None, :]
    x1, h2t = _mixer_call(
        sinks[0], x, positions.astype(f32)[..., None], mod, norm1_g, win, gq, gk, freq, sgn,
        jnp.asarray(_MSAME, bf16), conv_w[0], attn_out_g, conv_out_g, wout, norm2_g)

    wq = peer_wq[0].T.astype(bf16)
    sk = peer_subkeys[0].reshape(2 * PEER_HEADS, PEER_KEYS, PEER_QDIM // 2).astype(bf16)
    rows, gate = _router_call(h2t, wq, sk)

    out = _experts_call(h2t, rows, gate, peer_u[0].astype(bf16), peer_v[0].T.astype(bf16),
                        x1.reshape(T, D), mod, S // TOK_TILE)
    return out.reshape(B, S, D)
```

```python
import numpy as np
import jax
import jax.numpy as jnp
from jax import lax
from jax.experimental import pallas as pl
from jax.experimental.pallas import tpu as pltpu

D_MODEL = 1024
HEAD_DIM = 64
HALF_DIM = HEAD_DIM // 2
N_Q_HEADS = 8
N_KV_HEADS = 2
ATTN_WIDTH = N_Q_HEADS * HEAD_DIM
KV_WIDTH = N_KV_HEADS * HEAD_DIM
WINDOW = 128
ROPE_THETA = 10000.0
CONV_WIDTH = D_MODEL - ATTN_WIDTH
CONV_K = 3
PEER_HEADS = 8
PEER_KEYS = 128
PEER_EXPERTS = PEER_KEYS * PEER_KEYS
PEER_TOPK = 16
PEER_QDIM = 256
N_MOD = 6
EPS = 1e-6
NEG = -1e30

LANES = 128
SUBLANES = 8
BF16_ROWS = 2 * SUBLANES
N_ROW_GROUPS = PEER_KEYS // SUBLANES
N_KEY_VREGS = PEER_KEYS // BF16_ROWS
VMEM_LIMIT = 56 * 1024 * 1024

Q_OFF = 0
K_OFF = ATTN_WIDTH
V_OFF = K_OFF + 2 * KV_WIDTH
CB_OFF = V_OFF + 2 * KV_WIDTH
CC_OFF = CB_OFF + CONV_WIDTH
CU_OFF = CC_OFF + CONV_WIDTH
PROJ_WIDTH = CU_OFF + CONV_WIDTH

SEQ_TILE = 512
ROUTER_TILE = 512
TOK_TILE = 512
EXP_TILE = 1024
PACKED_HEADS = 6

_NN = (((1,), (0,)), ((), ()))
_NT = (((1,), (1,)), ((), ()))

f32 = jnp.float32
bf16 = jnp.bfloat16


def _split(x):
    hi = x.astype(bf16)
    lo = (x - hi.astype(f32)).astype(bf16)
    return hi, lo


def _dg(a, b, dims=_NN):
    return lax.dot_general(a, b, dims, preferred_element_type=f32)


def _dot3(a, b, dims=_NN):
    return _dg(a[0], b[0], dims) + (_dg(a[0], b[1], dims) + _dg(a[1], b[0], dims))


def _rms(x):
    return x * lax.rsqrt(jnp.mean(x * x, axis=-1, keepdims=True) + EPS)


def _mod_kernel(c_ref, w_ref, b_ref, o_ref):
    c = c_ref[...]
    ca = c * (1.0 / (1.0 + jnp.exp(-c)))
    o_ref[...] = _dot3(_split(ca), _split(w_ref[...])) + b_ref[...]


def _mod_call(c_pad, ada_w, ada_b):
    nb = 1536
    n = ada_w.shape[1]
    return pl.pallas_call(
        _mod_kernel,
        grid=(n // nb,),
        in_specs=[
            pl.BlockSpec((SUBLANES, D_MODEL), lambda j: (0, 0)),
            pl.BlockSpec((D_MODEL, nb), lambda j: (0, j)),
            pl.BlockSpec((1, nb), lambda j: (0, j)),
        ],
        out_specs=pl.BlockSpec((SUBLANES, nb), lambda j: (0, j)),
        out_shape=jax.ShapeDtypeStruct((SUBLANES, n), f32),
        compiler_params=pltpu.CompilerParams(
            dimension_semantics=("arbitrary",), vmem_limit_bytes=VMEM_LIMIT),
        name="mod",
    )(c_pad, ada_w, ada_b)


def _mixer_kernel(sink_ref, x_ref, pos_ref, mod_ref, g1_ref, win_ref, gq_ref, gk_ref,
                  freq_ref, sgn_ref, msame_ref, convw_ref, ga_ref, gc_ref, wout_ref,
                  g2_ref, x1_ref, h2t_ref, qbuf, kbuf, vbuf, abuf, ucarry):
    ts = x_ref.shape[0]
    s_idx = pl.program_id(1)
    x = x_ref[...]
    mod = mod_ref[...]
    shift1, scale1, gate1 = mod[0:1], mod[1:2], mod[2:3]
    shift2, scale2 = mod[3:4], mod[4:5]

    h = _rms(x) * g1_ref[...]
    hb = (h * (1.0 + scale1) + shift1).astype(bf16)

    def proj(c0, c1):
        return _dg(hb, win_ref[:, c0:c1])

    ang = pos_ref[...] * freq_ref[...]
    cosf = jnp.cos(ang)
    sins = jnp.sin(ang) * sgn_ref[...]
    msame = msame_ref[...]

    def headnorm_rope(t, g):
        sq = _split(t * t)
        ss = _dg(sq[0], msame) + _dg(sq[1], msame)
        tn = t * lax.rsqrt(ss * (1.0 / HEAD_DIM) + EPS) * g
        return tn * cosf + pltpu.roll(tn, LANES // 2, 1) * sins

    @pl.when(s_idx == 0)
    def _():
        kbuf[0:WINDOW, :] = jnp.zeros((WINDOW, 2 * LANES), f32)
        vbuf[0:WINDOW, :] = jnp.zeros((WINDOW, 2 * LANES), f32)
        ucarry[...] = jnp.zeros(ucarry.shape, f32)

    for c in range(ATTN_WIDTH // LANES):
        sl = slice(c * LANES, (c + 1) * LANES)
        qbuf[:, sl] = headnorm_rope(proj(Q_OFF + c * LANES, Q_OFF + (c + 1) * LANES), gq_ref[:, sl])
    for g in range(N_KV_HEADS):
        sl = slice(g * LANES, (g + 1) * LANES)
        kbuf[WINDOW:WINDOW + ts, sl] = headnorm_rope(
            proj(K_OFF + g * LANES, K_OFF + (g + 1) * LANES), gk_ref[:, sl])
    vbuf[WINDOW:WINDOW + ts, :] = proj(V_OFF, V_OFF + 2 * LANES)

    qi = lax.broadcasted_iota(jnp.int32, (WINDOW, 2 * WINDOW), 0)
    kj = lax.broadcasted_iota(jnp.int32, (WINDOW, 2 * WINDOW), 1)
    band = (kj > qi) & (kj <= qi + WINDOW)
    lane = lax.broadcasted_iota(jnp.int32, (1, LANES), 1)

    def attn_block(n, carry):
        r0 = pl.multiple_of(n * WINDOW, WINDOW)
        qb = qbuf[pl.ds(r0, WINDOW), :]
        kb = kbuf[pl.ds(r0, 2 * WINDOW), :]
        vb = vbuf[pl.ds(r0, 2 * WINDOW), :]
        first = jnp.logical_and(s_idx == 0, n == 0)
        valid = band & (kj >= jnp.where(first, WINDOW, 0))
        q_per_kv = N_Q_HEADS // N_KV_HEADS
        for g in range(N_KV_HEADS):
            kt = kb[:, g * LANES:(g + 1) * LANES].astype(bf16)
            vt = vb[:, g * LANES:(g + 1) * LANES].astype(bf16)
            qs = []
            for c in (2 * g, 2 * g + 1):
                qt = qb[:, c * LANES:(c + 1) * LANES]
                for sub in range(2):
                    qs.append(jnp.where((lane // HALF_DIM) % 2 == sub, qt, 0.0).astype(bf16))
            s_all = _dg(jnp.concatenate(qs, axis=0), kt, _NT) * (HEAD_DIM ** -0.5)
            ps = []
            for hh in range(q_per_kv):
                s = jnp.where(valid, s_all[hh * WINDOW:(hh + 1) * WINDOW, :], NEG)
                sink = sink_ref[q_per_kv * g + hh]
                m = jnp.maximum(jnp.max(s, axis=-1, keepdims=True), sink)
                e = jnp.exp(s - m)
                den = jnp.sum(e, axis=-1, keepdims=True) + jnp.exp(sink - m)
                ps.append((e / den).astype(bf16))
            o_all = _dg(jnp.concatenate(ps, axis=0), vt)
            for cc in range(2):
                o = jnp.where(lane < HEAD_DIM,
                              o_all[2 * cc * WINDOW:(2 * cc + 1) * WINDOW, :],
                              o_all[(2 * cc + 1) * WINDOW:(2 * cc + 2) * WINDOW, :])
                c = 2 * g + cc
                abuf[pl.ds(r0, WINDOW), c * LANES:(c + 1) * LANES] = o
        return carry

    lax.fori_loop(0, ts // WINDOW, attn_block, 0)
    kbuf[0:WINDOW, :] = kbuf[ts:ts + WINDOW, :]
    vbuf[0:WINDOW, :] = vbuf[ts:ts + WINDOW, :]

    cb = proj(CB_OFF, CB_OFF + CONV_WIDTH)
    u = proj(CC_OFF, CC_OFF + CONV_WIDTH) * proj(CU_OFF, CU_OFF + CONV_WIDTH)
    rowi = lax.broadcasted_iota(jnp.int32, u.shape, 0)
    prev1 = ucarry[SUBLANES - 1:SUBLANES, :]
    prev2 = ucarry[SUBLANES - 2:SUBLANES - 1, :]
    u1 = jnp.where(rowi == 0, prev1, pltpu.roll(u, 1, 0))
    u2 = jnp.where(rowi == 0, prev2, jnp.where(rowi == 1, prev1, pltpu.roll(u, 2, 0)))
    ucarry[...] = u[ts - SUBLANES:ts, :]
    w = convw_ref[...]
    conv = cb * (w[0:1] * u2 + w[1:2] * u1 + w[2:3] * u)

    ra = (_rms(abuf[...]) * ga_ref[...]).astype(bf16)
    rc = (_rms(conv) * gc_ref[...]).astype(bf16)
    y = _dg(ra, wout_ref[0:ATTN_WIDTH, :]) + _dg(rc, wout_ref[ATTN_WIDTH:, :])
    x1 = x + gate1 * y
    x1_ref[...] = x1
    h2 = _rms(x1) * g2_ref[...]
    h2t_ref[...] = (h2 * (1.0 + scale2) + shift2).T.astype(bf16)


def _const_spec(shape):
    nd = len(shape)
    return pl.BlockSpec(shape, lambda *_: (0,) * nd, pipeline_mode=pl.Buffered(1))


def _mixer_call(sinks, x, pos, mod, g1, win, gq, gk, freq, sgn, msame, convw, ga, gc, wout, g2):
    B, S, D = x.shape
    ts = SEQ_TILE
    nst = S // ts
    return pl.pallas_call(
        _mixer_kernel,
        grid=(B, nst),
        in_specs=[
            pl.BlockSpec(memory_space=pltpu.SMEM),
            pl.BlockSpec((None, ts, D), lambda b, s: (b, s, 0)),
            pl.BlockSpec((None, ts, 1), lambda b, s: (b, s, 0)),
            pl.BlockSpec((None, N_MOD, D), lambda b, s: (b, 0, 0)),
            _const_spec(g1.shape), _const_spec(win.shape),
            _const_spec(gq.shape), _const_spec(gk.shape), _const_spec(freq.shape),
            _const_spec(sgn.shape), _const_spec(msame.shape), _const_spec(convw.shape),
            _const_spec(ga.shape), _const_spec(gc.shape), _const_spec(wout.shape),
            _const_spec(g2.shape),
        ],
        out_specs=[
            pl.BlockSpec((None, ts, D), lambda b, s: (b, s, 0)),
            pl.BlockSpec((D, ts), lambda b, s: (0, b * nst + s)),
        ],
        out_shape=[
            jax.ShapeDtypeStruct((B, S, D), f32),
            jax.ShapeDtypeStruct((D, B * S), bf16),
        ],
        scratch_shapes=[
            pltpu.VMEM((ts, ATTN_WIDTH), f32),
            pltpu.VMEM((WINDOW + ts, 2 * LANES), f32),
            pltpu.VMEM((WINDOW + ts, 2 * LANES), f32),
            pltpu.VMEM((ts, ATTN_WIDTH), f32),
            pltpu.VMEM((SUBLANES, CONV_WIDTH), f32),
        ],
        compiler_params=pltpu.CompilerParams(
            dimension_semantics=("arbitrary", "arbitrary"), vmem_limit_bytes=VMEM_LIMIT),
        name="mixer",
    )(sinks, x, pos, mod, g1, win, gq, gk, freq, sgn, msame, convw, ga, gc, wout, g2)


def _oddeven_merge_sort_pairs(n):
    pairs = []
    p = 1
    while p < n:
        k = p
        while k >= 1:
            for j in range(k % p, n - k, 2 * k):
                for i in range(min(k, n - j - k)):
                    if (i + j) // (2 * p) == (i + j + k) // (2 * p):
                        pairs.append((i + j, i + j + k))
            k //= 2
        p *= 2
    return pairs


_SORT16 = _oddeven_merge_sort_pairs(PEER_TOPK)


def _bitonic_desc(z):
    z = list(z)
    d = PEER_TOPK // 2
    while d >= 1:
        for r in range(PEER_TOPK):
            if not r & d:
                hi, lo = jnp.maximum(z[r], z[r + d]), jnp.minimum(z[r], z[r + d])
                z[r], z[r + d] = hi, lo
        d //= 2
    return z


def _merge_top(R, L):
    z = list(R)
    for r in range(PEER_TOPK - len(L), PEER_TOPK):
        z[r] = jnp.maximum(R[r], L[PEER_TOPK - 1 - r])
    return _bitonic_desc(z)


def _top16_rows(sc):
    x = [sc[SUBLANES * g:SUBLANES * (g + 1), :] for g in range(PEER_KEYS // SUBLANES)]
    for i, j in _SORT16:
        x[i], x[j] = jnp.maximum(x[i], x[j]), jnp.minimum(x[i], x[j])
    for shift in (4, 2, 1):
        z = [jnp.maximum(x[r], pltpu.roll(x[PEER_TOPK - 1 - r], shift, 0)) for r in range(PEER_TOPK)]
        x = _bitonic_desc(z)
    return x


def _prefix_count(test, rows):
    assert len(rows) == PEER_TOPK == 16
    sel = jnp.where
    p3 = test(rows[7])
    p2 = test(sel(p3, rows[11], rows[3]))
    p1 = test(sel(p3, sel(p2, rows[13], rows[9]), sel(p2, rows[5], rows[1])))
    hi = sel(p2, sel(p1, rows[14], rows[12]), sel(p1, rows[10], rows[8]))
    lo = sel(p2, sel(p1, rows[6], rows[4]), sel(p1, rows[2], rows[0]))
    p0 = test(sel(p3, hi, lo))
    bits = ((p3, 8.0), (p2, 4.0), (p1, 2.0), (p0, 1.0), (test(rows[15]), 1.0))
    return sum(sel(p, v, 0.0) for p, v in bits)


def _router_kernel(h2t_ref, wq_ref, sk_ref, rows_ref, gate_ref, q_s, sc_s, top_s, row_s):
    tr = h2t_ref.shape[1]
    q_s[...] = _dg(wq_ref[...], h2t_ref[...])
    top_s[...] = jnp.zeros(top_s.shape, f32)
    sub = lax.broadcasted_iota(jnp.int32, (SUBLANES, tr), 0)

    def head_body(h, carry):
        for p in range(2):
            hp = 2 * h + p
            r0 = pl.multiple_of(hp * PEER_KEYS, PEER_KEYS)
            sc = _dg(sk_ref[hp], q_s[pl.ds(r0, PEER_KEYS), :].astype(bf16))
            sc_s[hp] = sc
            top = _top16_rows(sc)
            for r in range(PEER_TOPK):
                top_s[p, r] = jnp.where(sub == h, top[r], top_s[p, r])
        return carry

    lax.fori_loop(0, PEER_HEADS, head_body, 0)

    a = [top_s[0, r] for r in range(PEER_TOPK)]
    b = [top_s[1, r] for r in range(PEER_TOPK)]
    R = [a[0] + b[r] for r in range(16)]
    R = _merge_top(R, [a[r] + b[0] for r in range(1, 16)])
    R = _merge_top(R, [a[1] + b[r] for r in range(1, 8)])
    R = _merge_top(R, [a[r] + b[1] for r in range(2, 8)])
    R = _merge_top(R, [a[2] + b[r] for r in range(2, 5)])
    R = _merge_top(R, [a[r] + b[2] for r in range(3, 5)])
    R = _merge_top(R, [a[3] + b[3]])
    tau = R[PEER_TOPK - 1]
    z = jnp.exp(R[0] - R[0])
    for r in range(1, PEER_TOPK):
        z = z + jnp.exp(R[r] - R[0])
    row_s[0] = a[0]
    row_s[1] = b[0]
    row_s[2] = 0.5 / z
    row_s[3] = tau

    def fac_body(h, carry):
        row = pl.ds(h, 1)
        s1 = sc_s[2 * h]
        s2 = sc_s[2 * h + 1]
        tau_h = row_s[3, row, :]
        b_rows = [top_s[1, r, row, :] for r in range(PEER_TOPK)]
        cnt = _prefix_count(lambda b_r: s1 + b_r >= tau_h, b_rows)
        rk = _prefix_count(lambda b_r: b_r > s2, b_rows)
        f1 = jnp.exp(s1 - row_s[0, row, :]) * row_s[2, row, :]
        f2 = jnp.exp(s2 - row_s[1, row, :])
        for tc in range(tr // LANES):
            cols = slice(tc * LANES, (tc + 1) * LANES)
            for ig in range(N_ROW_GROUPS):
                keys = slice(ig * SUBLANES, (ig + 1) * SUBLANES)
                rows_ref[tc, ig, h, 0] = f1[keys, cols]
                rows_ref[tc, ig, h, 1] = cnt[keys, cols]
            for v in range(N_KEY_VREGS):
                keys = slice(v * BF16_ROWS, (v + 1) * BF16_ROWS)
                gate_ref[tc, v, h, 0] = rk[keys, cols]
                gate_ref[tc, v, h, 1] = f2[keys, cols]
        return carry

    lax.fori_loop(0, PEER_HEADS, fac_body, 0)


def _router_call(h2t, wq, sk):
    D, T = h2t.shape
    tr = ROUTER_TILE
    rows_shape = (N_ROW_GROUPS, PEER_HEADS, 2, SUBLANES, LANES)
    gate_shape = (N_KEY_VREGS, PEER_HEADS, 2, BF16_ROWS, LANES)
    lead = lambda i: (i, 0, 0, 0, 0, 0)
    return pl.pallas_call(
        _router_kernel,
        grid=(T // tr,),
        in_specs=[
            pl.BlockSpec((D, tr), lambda i: (0, i)),
            _const_spec(wq.shape), _const_spec(sk.shape),
        ],
        out_specs=[
            pl.BlockSpec((tr // LANES,) + rows_shape, lead),
            pl.BlockSpec((tr // LANES,) + gate_shape, lead),
        ],
        out_shape=[
            jax.ShapeDtypeStruct((T // LANES,) + rows_shape, f32),
            jax.ShapeDtypeStruct((T // LANES,) + gate_shape, f32),
        ],
        scratch_shapes=[
            pltpu.VMEM((PEER_HEADS * PEER_QDIM, tr), f32),
            pltpu.VMEM((2 * PEER_HEADS, PEER_KEYS, tr), f32),
            pltpu.VMEM((2, PEER_TOPK, SUBLANES, tr), f32),
            pltpu.VMEM((4, SUBLANES, tr), f32),
        ],
        compiler_params=pltpu.CompilerParams(
            dimension_semantics=("arbitrary",), vmem_limit_bytes=VMEM_LIMIT),
        name="router",
    )(h2t, wq, sk)


_GELU_C1 = float(np.sqrt(2.0 / np.pi))
_GELU_C2 = 0.044715 * _GELU_C1


def _experts_kernel(h2b_ref, rows_ref, gate_ref, u_ref, vt_ref, x1_ref, mod_ref,
                    out_ref, acc, a_s, w_s, gate_s):
    tm = h2b_ref.shape[1]
    te = u_ref.shape[0]
    assert te == SUBLANES * PEER_KEYS
    j = pl.program_id(1)

    @pl.when(j == 0)
    def _():
        acc[...] = jnp.zeros(acc.shape, f32)
        gate_s[...] = gate_ref[...].astype(bf16)

    a_s[:, 0:tm] = _dg(u_ref[...], h2b_ref[...])

    def packed_row(tc, h, kind, r):
        row = rows_ref[tc, j, h, kind][r:r + 1, :]
        return jnp.broadcast_to(row, (BF16_ROWS, LANES)).astype(bf16)

    for r in range(SUBLANES):
        for tc in range(tm // LANES):
            cols = slice(tc * LANES, (tc + 1) * LANES)
            f1 = [packed_row(tc, h, 0, r) for h in range(PACKED_HEADS)]
            cnt = [packed_row(tc, h, 1, r) for h in range(PACKED_HEADS)]
            f1_w = [rows_ref[tc, j, h, 0][r:r + 1, :] for h in range(PACKED_HEADS, PEER_HEADS)]
            cnt_w = [rows_ref[tc, j, h, 1][r:r + 1, :] for h in range(PACKED_HEADS, PEER_HEADS)]
            for v in range(N_KEY_VREGS):
                rows = slice(r * PEER_KEYS + v * BF16_ROWS, r * PEER_KEYS + (v + 1) * BF16_ROWS)
                g = jnp.zeros((BF16_ROWS, LANES), bf16)
                for h in range(PACKED_HEADS):
                    sel = jnp.where(gate_s[tc, v, h, 0] < cnt[h], gate_s[tc, v, h, 1],
                                    jnp.zeros((), bf16))
                    g = g + sel * f1[h]
                g_w = jnp.zeros((BF16_ROWS, LANES), f32)
                for k, h in enumerate(range(PACKED_HEADS, PEER_HEADS)):
                    sel = jnp.where(gate_ref[tc, v, h, 0] < cnt_w[k], gate_ref[tc, v, h, 1], 0.0)
                    g_w = g_w + sel * f1_w[k]
                a = a_s[rows, cols]
                t = jnp.tanh(a * (_GELU_C1 + _GELU_C2 * (a * a)))
                w_s[rows, cols] = ((g.astype(f32) + g_w) * (a * (1.0 + t))).astype(bf16)

    acc[...] += _dg(vt_ref[...], w_s[:, 0:tm])

    @pl.when(j == pl.num_programs(1) - 1)
    def _():
        gate2 = mod_ref[N_MOD - 1:N_MOD, :]
        out_ref[...] = x1_ref[...] + gate2 * acc[...].T


def _experts_call(h2b, rows, gate, u_b, vt_b, x1, mod, tiles_per_batch):
    D, T = h2b.shape
    tm, te = TOK_TILE, EXP_TILE
    lead = lambda i, j: (i, 0, 0, 0, 0, 0)
    return pl.pallas_call(
        _experts_kernel,
        grid=(T // tm, PEER_EXPERTS // te),
        in_specs=[
            pl.BlockSpec((D, tm), lambda i, j: (0, i)),
            pl.BlockSpec((tm // LANES,) + rows.shape[1:], lead),
            pl.BlockSpec((tm // LANES,) + gate.shape[1:], lead),
            pl.BlockSpec((te, D), lambda i, j: (j, 0)),
            pl.BlockSpec((D, te), lambda i, j: (0, j)),
            pl.BlockSpec((tm, D), lambda i, j: (i, 0)),
            pl.BlockSpec((None, N_MOD, D), lambda i, j: (i // tiles_per_batch, 0, 0)),
        ],
        out_specs=pl.BlockSpec((tm, D), lambda i, j: (i, 0)),
        out_shape=jax.ShapeDtypeStruct((T, D), f32),
        scratch_shapes=[
            pltpu.VMEM((D, tm), f32),
            pltpu.VMEM((te, tm + LANES), f32),
            pltpu.VMEM((te, tm + LANES), bf16),
            pltpu.VMEM((tm // LANES,) + gate.shape[1:], bf16),
        ],
        compiler_params=pltpu.CompilerParams(
            dimension_semantics=("arbitrary", "arbitrary"), vmem_limit_bytes=VMEM_LIMIT),
        name="experts",
    )(h2b, rows, gate, u_b, vt_b, x1, mod)


def _proj_columns():
    half = np.arange(HALF_DIM)
    cols = []
    for c in range(N_Q_HEADS // 2):
        for part in range(2):
            for head in (2 * c, 2 * c + 1):
                cols.append(head * HEAD_DIM + part * HALF_DIM + half)
    for g in range(N_KV_HEADS):
        for part in (0, 0, 1, 1):
            cols.append(ATTN_WIDTH + g * HEAD_DIM + part * HALF_DIM + half)
    for g in range(N_KV_HEADS):
        for _ in range(2):
            cols.append(ATTN_WIDTH + KV_WIDTH + g * HEAD_DIM + np.arange(HEAD_DIM))
    cols.append(np.arange(ATTN_WIDTH + 2 * KV_WIDTH, ATTN_WIDTH + 2 * KV_WIDTH + 3 * CONV_WIDTH))
    return np.concatenate(cols)


_PROJ_COLS = _proj_columns()
_TILE_DIM = np.concatenate([np.arange(HALF_DIM), np.arange(HALF_DIM),
                            HALF_DIM + np.arange(HALF_DIM), HALF_DIM + np.arange(HALF_DIM)])
_TILE_HEAD = (np.arange(LANES) // HALF_DIM) % 2
_MSAME = (_TILE_HEAD[:, None] == _TILE_HEAD[None, :]).astype(np.float32)


def kernel(x, c, positions, ada_w, ada_b, norm1_g, w_in, q_norm_g, k_norm_g, sinks, conv_w,
           attn_out_g, conv_out_g, w_out, norm2_g, peer_wq, peer_subkeys, peer_u, peer_v):
    B, S, D = x.shape
    T = B * S
    assert ada_w.shape[0] == 1, "single-layer block"
    assert D == D_MODEL and S % SEQ_TILE == 0 and T % TOK_TILE == 0 and T % ROUTER_TILE == 0
    assert S % TOK_TILE == 0

    c_pad = jnp.pad(c, ((0, SUBLANES - B), (0, 0)))
    mod = _mod_call(c_pad, ada_w[0], ada_b[0][None, :])[:B].reshape(B, N_MOD, D)

    inv_freq = ROPE_THETA ** (-jnp.arange(0, HEAD_DIM, 2, dtype=f32) / HEAD_DIM)
    freq = inv_freq[_TILE_DIM % HALF_DIM][None, :]
    sgn = jnp.asarray(np.where(np.arange(LANES) < LANES // 2, -1.0, 1.0), f32)[None, :]
    win = w_in[0][:, _PROJ_COLS].astype(bf16)
    wout = w_out[0].astype(bf16)
    gq = jnp.tile(q_norm_g[0][_TILE_DIM], ATTN_WIDTH // LANES)[None, :]
    gk = jnp.tile(k_norm_g[0][_TILE_DIM], N_KV_HEADS)[None, :]
    x1, h2t = _mixer_call(
        sinks[0], x, positions.astype(f32)[..., None], mod, norm1_g, win, gq, gk, freq, sgn,
        jnp.asarray(_MSAME, bf16), conv_w[0], attn_out_g, conv_out_g, wout, norm2_g)

    wq = peer_wq[0].T.astype(bf16)
    sk = peer_subkeys[0].reshape(2 * PEER_HEADS, PEER_KEYS, PEER_QDIM // 2).astype(bf16)
    rows, gate = _router_call(h2t, wq, sk)

    out = _experts_call(h2t, rows, gate, peer_u[0].astype(bf16), peer_v[0].T.astype(bf16),
                        x1.reshape(T, D), mod, S // TOK_TILE)
    return out.reshape(B, S, D)
```

```python
import numpy as np
import jax
import jax.numpy as jnp
from jax import lax
from jax.experimental import pallas as pl
from jax.experimental.pallas import tpu as pltpu

D_MODEL = 1024
HEAD_DIM = 64
HALF_DIM = HEAD_DIM // 2
N_Q_HEADS = 8
N_KV_HEADS = 2
ATTN_WIDTH = N_Q_HEADS * HEAD_DIM
KV_WIDTH = N_KV_HEADS * HEAD_DIM
WINDOW = 128
ROPE_THETA = 10000.0
CONV_WIDTH = D_MODEL - ATTN_WIDTH
CONV_K = 3
PEER_HEADS = 8
PEER_KEYS = 128
PEER_EXPERTS = PEER_KEYS * PEER_KEYS
PEER_TOPK = 16
PEER_QDIM = 256
N_MOD = 6
EPS = 1e-6
NEG = -1e30

LANES = 128
SUBLANES = 8
BF16_ROWS = 2 * SUBLANES
N_ROW_GROUPS = PEER_KEYS // SUBLANES
N_KEY_VREGS = PEER_KEYS // BF16_ROWS
VMEM_LIMIT = 56 * 1024 * 1024

Q_OFF = 0
K_OFF = ATTN_WIDTH
V_OFF = K_OFF + 2 * KV_WIDTH
CB_OFF = V_OFF + 2 * KV_WIDTH
CC_OFF = CB_OFF + CONV_WIDTH
CU_OFF = CC_OFF + CONV_WIDTH
PROJ_WIDTH = CU_OFF + CONV_WIDTH

SEQ_TILE = 512
ROUTER_TILE = 512
TOK_TILE = 512
EXP_TILE = 2048

_NN = (((1,), (0,)), ((), ()))
_NT = (((1,), (1,)), ((), ()))

f32 = jnp.float32
bf16 = jnp.bfloat16


def _split(x):
    hi = x.astype(bf16)
    lo = (x - hi.astype(f32)).astype(bf16)
    return hi, lo


def _dg(a, b, dims=_NN):
    return lax.dot_general(a, b, dims, preferred_element_type=f32)


def _dot3(a, b, dims=_NN):
    return _dg(a[0], b[0], dims) + (_dg(a[0], b[1], dims) + _dg(a[1], b[0], dims))


def _rms(x):
    return x * lax.rsqrt(jnp.mean(x * x, axis=-1, keepdims=True) + EPS)


def _mod_kernel(c_ref, w_ref, b_ref, o_ref):
    c = c_ref[...]
    ca = c * (1.0 / (1.0 + jnp.exp(-c)))
    o_ref[...] = _dot3(_split(ca), _split(w_ref[...])) + b_ref[...]


def _mod_call(c_pad, ada_w, ada_b):
    nb = 1536
    n = ada_w.shape[1]
    return pl.pallas_call(
        _mod_kernel,
        grid=(n // nb,),
        in_specs=[
            pl.BlockSpec((SUBLANES, D_MODEL), lambda j: (0, 0)),
            pl.BlockSpec((D_MODEL, nb), lambda j: (0, j)),
            pl.BlockSpec((1, nb), lambda j: (0, j)),
        ],
        out_specs=pl.BlockSpec((SUBLANES, nb), lambda j: (0, j)),
        out_shape=jax.ShapeDtypeStruct((SUBLANES, n), f32),
        compiler_params=pltpu.CompilerParams(
            dimension_semantics=("arbitrary",), vmem_limit_bytes=VMEM_LIMIT),
        name="mod",
    )(c_pad, ada_w, ada_b)


def _mixer_kernel(sink_ref, x_ref, pos_ref, mod_ref, g1_ref, win_ref, gq_ref, gk_ref,
                  freq_ref, sgn_ref, msame_ref, convw_ref, ga_ref, gc_ref, wout_ref,
                  g2_ref, x1_ref, h2t_ref, qbuf, kbuf, vbuf, abuf, ucarry):
    ts = x_ref.shape[0]
    s_idx = pl.program_id(1)
    x = x_ref[...]
    mod = mod_ref[...]
    shift1, scale1, gate1 = mod[0:1], mod[1:2], mod[2:3]
    shift2, scale2 = mod[3:4], mod[4:5]

    h = _rms(x) * g1_ref[...]
    hb = (h * (1.0 + scale1) + shift1).astype(bf16)

    def proj(c0, c1):
        return _dg(hb, win_ref[:, c0:c1])

    ang = pos_ref[...] * freq_ref[...]
    cosf = jnp.cos(ang)
    sins = jnp.sin(ang) * sgn_ref[...]
    msame = msame_ref[...]

    def headnorm_rope(t, g):
        sq = _split(t * t)
        ss = _dg(sq[0], msame) + _dg(sq[1], msame)
        tn = t * lax.rsqrt(ss * (1.0 / HEAD_DIM) + EPS) * g
        return tn * cosf + pltpu.roll(tn, LANES // 2, 1) * sins

    @pl.when(s_idx == 0)
    def _():
        kbuf[0:WINDOW, :] = jnp.zeros((WINDOW, 2 * LANES), f32)
        vbuf[0:WINDOW, :] = jnp.zeros((WINDOW, 2 * LANES), f32)
        ucarry[...] = jnp.zeros(ucarry.shape, f32)

    for c in range(ATTN_WIDTH // LANES):
        sl = slice(c * LANES, (c + 1) * LANES)
        qbuf[:, sl] = headnorm_rope(proj(Q_OFF + c * LANES, Q_OFF + (c + 1) * LANES), gq_ref[:, sl])
    for g in range(N_KV_HEADS):
        sl = slice(g * LANES, (g + 1) * LANES)
        kbuf[WINDOW:WINDOW + ts, sl] = headnorm_rope(
            proj(K_OFF + g * LANES, K_OFF + (g + 1) * LANES), gk_ref[:, sl])
    vbuf[WINDOW:WINDOW + ts, :] = proj(V_OFF, V_OFF + 2 * LANES)

    qi = lax.broadcasted_iota(jnp.int32, (WINDOW, 2 * WINDOW), 0)
    kj = lax.broadcasted_iota(jnp.int32, (WINDOW, 2 * WINDOW), 1)
    band = (kj > qi) & (kj <= qi + WINDOW)
    lane = lax.broadcasted_iota(jnp.int32, (1, LANES), 1)

    def attn_block(n, carry):
        r0 = pl.multiple_of(n * WINDOW, WINDOW)
        qb = qbuf[pl.ds(r0, WINDOW), :]
        kb = kbuf[pl.ds(r0, 2 * WINDOW), :]
        vb = vbuf[pl.ds(r0, 2 * WINDOW), :]
        first = jnp.logical_and(s_idx == 0, n == 0)
        valid = band & (kj >= jnp.where(first, WINDOW, 0))
        q_per_kv = N_Q_HEADS // N_KV_HEADS
        for g in range(N_KV_HEADS):
            kt = kb[:, g * LANES:(g + 1) * LANES].astype(bf16)
            vt = vb[:, g * LANES:(g + 1) * LANES].astype(bf16)
            qs = []
            for c in (2 * g, 2 * g + 1):
                qt = qb[:, c * LANES:(c + 1) * LANES]
                for sub in range(2):
                    qs.append(jnp.where((lane // HALF_DIM) % 2 == sub, qt, 0.0).astype(bf16))
            s_all = _dg(jnp.concatenate(qs, axis=0), kt, _NT) * (HEAD_DIM ** -0.5)
            ps = []
            for hh in range(q_per_kv):
                s = jnp.where(valid, s_all[hh * WINDOW:(hh + 1) * WINDOW, :], NEG)
                sink = sink_ref[q_per_kv * g + hh]
                m = jnp.maximum(jnp.max(s, axis=-1, keepdims=True), sink)
                e = jnp.exp(s - m)
                den = jnp.sum(e, axis=-1, keepdims=True) + jnp.exp(sink - m)
                ps.append((e / den).astype(bf16))
            o_all = _dg(jnp.concatenate(ps, axis=0), vt)
            for cc in range(2):
                o = jnp.where(lane < HEAD_DIM,
                              o_all[2 * cc * WINDOW:(2 * cc + 1) * WINDOW, :],
                              o_all[(2 * cc + 1) * WINDOW:(2 * cc + 2) * WINDOW, :])
                c = 2 * g + cc
                abuf[pl.ds(r0, WINDOW), c * LANES:(c + 1) * LANES] = o
        return carry

    lax.fori_loop(0, ts // WINDOW, attn_block, 0)
    kbuf[0:WINDOW, :] = kbuf[ts:ts + WINDOW, :]
    vbuf[0:WINDOW, :] = vbuf[ts:ts + WINDOW, :]

    cb = proj(CB_OFF, CB_OFF + CONV_WIDTH)
    u = proj(CC_OFF, CC_OFF + CONV_WIDTH) * proj(CU_OFF, CU_OFF + CONV_WIDTH)
    rowi = lax.broadcasted_iota(jnp.int32, u.shape, 0)
    prev1 = ucarry[SUBLANES - 1:SUBLANES, :]
    prev2 = ucarry[SUBLANES - 2:SUBLANES - 1, :]
    u1 = jnp.where(rowi == 0, prev1, pltpu.roll(u, 1, 0))
    u2 = jnp.where(rowi == 0, prev2, jnp.where(rowi == 1, prev1, pltpu.roll(u, 2, 0)))
    ucarry[...] = u[ts - SUBLANES:ts, :]
    w = convw_ref[...]
    conv = cb * (w[0:1] * u2 + w[1:2] * u1 + w[2:3] * u)

    ra = (_rms(abuf[...]) * ga_ref[...]).astype(bf16)
    rc = (_rms(conv) * gc_ref[...]).astype(bf16)
    y = _dg(ra, wout_ref[0:ATTN_WIDTH, :]) + _dg(rc, wout_ref[ATTN_WIDTH:, :])
    x1 = x + gate1 * y
    x1_ref[...] = x1
    h2 = _rms(x1) * g2_ref[...]
    h2t_ref[...] = (h2 * (1.0 + scale2) + shift2).T.astype(bf16)


def _const_spec(shape):
    nd = len(shape)
    return pl.BlockSpec(shape, lambda *_: (0,) * nd, pipeline_mode=pl.Buffered(1))


def _mixer_call(sinks, x, pos, mod, g1, win, gq, gk, freq, sgn, msame, convw, ga, gc, wout, g2):
    B, S, D = x.shape
    ts = SEQ_TILE
    nst = S // ts
    return pl.pallas_call(
        _mixer_kernel,
        grid=(B, nst),
        in_specs=[
            pl.BlockSpec(memory_space=pltpu.SMEM),
            pl.BlockSpec((None, ts, D), lambda b, s: (b, s, 0)),
            pl.BlockSpec((None, ts, 1), lambda b, s: (b, s, 0)),
            pl.BlockSpec((None, N_MOD, D), lambda b, s: (b, 0, 0)),
            _const_spec(g1.shape), _const_spec(win.shape),
            _const_spec(gq.shape), _const_spec(gk.shape), _const_spec(freq.shape),
            _const_spec(sgn.shape), _const_spec(msame.shape), _const_spec(convw.shape),
            _const_spec(ga.shape), _const_spec(gc.shape), _const_spec(wout.shape),
            _const_spec(g2.shape),
        ],
        out_specs=[
            pl.BlockSpec((None, ts, D), lambda b, s: (b, s, 0)),
            pl.BlockSpec((D, ts), lambda b, s: (0, b * nst + s)),
        ],
        out_shape=[
            jax.ShapeDtypeStruct((B, S, D), f32),
            jax.ShapeDtypeStruct((D, B * S), bf16),
        ],
        scratch_shapes=[
            pltpu.VMEM((ts, ATTN_WIDTH), f32),
            pltpu.VMEM((WINDOW + ts, 2 * LANES), f32),
            pltpu.VMEM((WINDOW + ts, 2 * LANES), f32),
            pltpu.VMEM((ts, ATTN_WIDTH), f32),
            pltpu.VMEM((SUBLANES, CONV_WIDTH), f32),
        ],
        compiler_params=pltpu.CompilerParams(
            dimension_semantics=("arbitrary", "arbitrary"), vmem_limit_bytes=VMEM_LIMIT),
        name="mixer",
    )(sinks, x, pos, mod, g1, win, gq, gk, freq, sgn, msame, convw, ga, gc, wout, g2)


def _oddeven_merge_sort_pairs(n):
    pairs = []
    p = 1
    while p < n:
        k = p
        while k >= 1:
            for j in range(k % p, n - k, 2 * k):
                for i in range(min(k, n - j - k)):
                    if (i + j) // (2 * p) == (i + j + k) // (2 * p):
                        pairs.append((i + j, i + j + k))
            k //= 2
        p *= 2
    return pairs


_SORT16 = _oddeven_merge_sort_pairs(PEER_TOPK)


def _bitonic_desc(z):
    z = list(z)
    d = PEER_TOPK // 2
    while d >= 1:
        for r in range(PEER_TOPK):
            if not r & d:
                hi, lo = jnp.maximum(z[r], z[r + d]), jnp.minimum(z[r], z[r + d])
                z[r], z[r + d] = hi, lo
        d //= 2
    return z


def _merge_top(R, L):
    z = list(R)
    for r in range(PEER_TOPK - len(L), PEER_TOPK):
        z[r] = jnp.maximum(R[r], L[PEER_TOPK - 1 - r])
    return _bitonic_desc(z)


def _top16_rows(sc):
    x = [sc[SUBLANES * g:SUBLANES * (g + 1), :] for g in range(PEER_KEYS // SUBLANES)]
    for i, j in _SORT16:
        x[i], x[j] = jnp.maximum(x[i], x[j]), jnp.minimum(x[i], x[j])
    for shift in (4, 2, 1):
        z = [jnp.maximum(x[r], pltpu.roll(x[PEER_TOPK - 1 - r], shift, 0)) for r in range(PEER_TOPK)]
        x = _bitonic_desc(z)
    return x


def _prefix_count(test, rows):
    assert len(rows) == PEER_TOPK == 16
    sel = jnp.where
    p3 = test(rows[7])
    p2 = test(sel(p3, rows[11], rows[3]))
    p1 = test(sel(p3, sel(p2, rows[13], rows[9]), sel(p2, rows[5], rows[1])))
    hi = sel(p2, sel(p1, rows[14], rows[12]), sel(p1, rows[10], rows[8]))
    lo = sel(p2, sel(p1, rows[6], rows[4]), sel(p1, rows[2], rows[0]))
    p0 = test(sel(p3, hi, lo))
    bits = ((p3, 8.0), (p2, 4.0), (p1, 2.0), (p0, 1.0), (test(rows[15]), 1.0))
    return sum(sel(p, v, 0.0) for p, v in bits)


def _router_kernel(h2t_ref, wq_ref, sk_ref, rows_ref, gate_ref, q_s, sc_s, top_s, row_s):
    tr = h2t_ref.shape[1]
    q_s[...] = _dg(wq_ref[...], h2t_ref[...])
    top_s[...] = jnp.zeros(top_s.shape, f32)
    sub = lax.broadcasted_iota(jnp.int32, (SUBLANES, tr), 0)

    def head_body(h, carry):
        for p in range(2):
            hp = 2 * h + p
            r0 = pl.multiple_of(hp * PEER_KEYS, PEER_KEYS)
            sc = _dg(sk_ref[hp], q_s[pl.ds(r0, PEER_KEYS), :].astype(bf16))
            sc_s[hp] = sc
            top = _top16_rows(sc)
            for r in range(PEER_TOPK):
                top_s[p, r] = jnp.where(sub == h, top[r], top_s[p, r])
        return carry

    lax.fori_loop(0, PEER_HEADS, head_body, 0)

    a = [top_s[0, r] for r in range(PEER_TOPK)]
    b = [top_s[1, r] for r in range(PEER_TOPK)]
    R = [a[0] + b[r] for r in range(16)]
    R = _merge_top(R, [a[r] + b[0] for r in range(1, 16)])
    R = _merge_top(R, [a[1] + b[r] for r in range(1, 8)])
    R = _merge_top(R, [a[r] + b[1] for r in range(2, 8)])
    R = _merge_top(R, [a[2] + b[r] for r in range(2, 5)])
    R = _merge_top(R, [a[r] + b[2] for r in range(3, 5)])
    R = _merge_top(R, [a[3] + b[3]])
    tau = R[PEER_TOPK - 1]
    z = jnp.exp(R[0] - R[0])
    for r in range(1, PEER_TOPK):
        z = z + jnp.exp(R[r] - R[0])
    row_s[0] = a[0]
    row_s[1] = b[0]
    row_s[2] = 0.5 / z
    row_s[3] = tau

    def fac_body(h, carry):
        row = pl.ds(h, 1)
        s1 = sc_s[2 * h]
        s2 = sc_s[2 * h + 1]
        tau_h = row_s[3, row, :]
        b_rows = [top_s[1, r, row, :] for r in range(PEER_TOPK)]
        cnt = _prefix_count(lambda b_r: s1 + b_r >= tau_h, b_rows)
        rk = _prefix_count(lambda b_r: b_r > s2, b_rows)
        f1 = jnp.exp(s1 - row_s[0, row, :]) * row_s[2, row, :]
        f2 = jnp.exp(s2 - row_s[1, row, :])
        for tc in range(tr // LANES):
            cols = slice(tc * LANES, (tc + 1) * LANES)
            for ig in range(N_ROW_GROUPS):
                keys = slice(ig * SUBLANES, (ig + 1) * SUBLANES)
                rows_ref[tc, ig, h, 0] = f1[keys, cols]
                rows_ref[tc, ig, h, 1] = cnt[keys, cols]
            for v in range(N_KEY_VREGS):
                keys = slice(v * BF16_ROWS, (v + 1) * BF16_ROWS)
                gate_ref[tc, v, h, 0] = rk[keys, cols]
                gate_ref[tc, v, h, 1] = f2[keys, cols]
        return carry

    lax.fori_loop(0, PEER_HEADS, fac_body, 0)


def _router_call(h2t, wq, sk):
    D, T = h2t.shape
    tr = ROUTER_TILE
    rows_shape = (N_ROW_GROUPS, PEER_HEADS, 2, SUBLANES, LANES)
    gate_shape = (N_KEY_VREGS, PEER_HEADS, 2, BF16_ROWS, LANES)
    lead = lambda i: (i, 0, 0, 0, 0, 0)
    return pl.pallas_call(
        _router_kernel,
        grid=(T // tr,),
        in_specs=[
            pl.BlockSpec((D, tr), lambda i: (0, i)),
            _const_spec(wq.shape), _const_spec(sk.shape),
        ],
        out_specs=[
            pl.BlockSpec((tr // LANES,) + rows_shape, lead),
            pl.BlockSpec((tr // LANES,) + gate_shape, lead),
        ],
        out_shape=[
            jax.ShapeDtypeStruct((T // LANES,) + rows_shape, f32),
            jax.ShapeDtypeStruct((T // LANES,) + gate_shape, f32),
        ],
        scratch_shapes=[
            pltpu.VMEM((PEER_HEADS * PEER_QDIM, tr), f32),
            pltpu.VMEM((2 * PEER_HEADS, PEER_KEYS, tr), f32),
            pltpu.VMEM((2, PEER_TOPK, SUBLANES, tr), f32),
            pltpu.VMEM((4, SUBLANES, tr), f32),
        ],
        compiler_params=pltpu.CompilerParams(
            dimension_semantics=("arbitrary",), vmem_limit_bytes=VMEM_LIMIT),
        name="router",
    )(h2t, wq, sk)


_GELU_C1 = float(np.sqrt(2.0 / np.pi))
_GELU_C2 = 0.044715 * _GELU_C1


def _experts_kernel(h2b_ref, rows_ref, gate_ref, u_ref, vt_ref, x1_ref, mod_ref,
                    out_ref, acc, a_s, w_s, gate_s):
    tm = h2b_ref.shape[1]
    te = u_ref.shape[0]
    assert te % (SUBLANES * PEER_KEYS) == 0
    groups = te // (SUBLANES * PEER_KEYS)
    j = pl.program_id(1)

    @pl.when(j == 0)
    def _():
        acc[...] = jnp.zeros(acc.shape, f32)
        gate_s[...] = gate_ref[...].astype(bf16)

    a_s[:, 0:tm] = _dg(u_ref[...], h2b_ref[...])

    def packed_row(tc, h, kind, r):
        row = rows_ref[tc, j * groups + r // SUBLANES, h, kind][r % SUBLANES:r % SUBLANES + 1, :]
        return jnp.broadcast_to(row, (BF16_ROWS, LANES)).astype(bf16)

    for r in range(te // PEER_KEYS):
        for tc in range(tm // LANES):
            cols = slice(tc * LANES, (tc + 1) * LANES)
            f1 = [packed_row(tc, h, 0, r) for h in range(PEER_HEADS)]
            cnt = [packed_row(tc, h, 1, r) for h in range(PEER_HEADS)]
            for v in range(N_KEY_VREGS):
                rows = slice(r * PEER_KEYS + v * BF16_ROWS, r * PEER_KEYS + (v + 1) * BF16_ROWS)
                g = jnp.zeros((BF16_ROWS, LANES), bf16)
                for h in range(PEER_HEADS):
                    sel = jnp.where(gate_s[tc, v, h, 0] < cnt[h], gate_s[tc, v, h, 1],
                                    jnp.zeros((), bf16))
                    g = g + sel * f1[h]
                a = a_s[rows, cols]
                t = jnp.tanh(a * (_GELU_C1 + _GELU_C2 * (a * a)))
                w_s[rows, cols] = g * (a * (1.0 + t)).astype(bf16)

    acc[...] += _dg(vt_ref[...], w_s[:, 0:tm])

    @pl.when(j == pl.num_programs(1) - 1)
    def _():
        gate2 = mod_ref[N_MOD - 1:N_MOD, :]
        out_ref[...] = x1_ref[...] + gate2 * acc[...].T


def _experts_call(h2b, rows, gate, u_b, vt_b, x1, mod, tiles_per_batch):
    D, T = h2b.shape
    tm, te = TOK_TILE, EXP_TILE
    lead = lambda i, j: (i, 0, 0, 0, 0, 0)
    return pl.pallas_call(
        _experts_kernel,
        grid=(T // tm, PEER_EXPERTS // te),
        in_specs=[
            pl.BlockSpec((D, tm), lambda i, j: (0, i)),
            pl.BlockSpec((tm // LANES,) + rows.shape[1:], lead),
            pl.BlockSpec((tm // LANES,) + gate.shape[1:], lead),
            pl.BlockSpec((te, D), lambda i, j: (j, 0)),
            pl.BlockSpec((D, te), lambda i, j: (0, j)),
            pl.BlockSpec((tm, D), lambda i, j: (i, 0)),
            pl.BlockSpec((None, N_MOD, D), lambda i, j: (i // tiles_per_batch, 0, 0)),
        ],
        out_specs=pl.BlockSpec((tm, D), lambda i, j: (i, 0)),
        out_shape=jax.ShapeDtypeStruct((T, D), f32),
        scratch_shapes=[
            pltpu.VMEM((D, tm), f32),
            pltpu.VMEM((te, tm + LANES), f32),
            pltpu.VMEM((te, tm + LANES), bf16),
            pltpu.VMEM((tm // LANES,) + gate.shape[1:], bf16),
        ],
        compiler_params=pltpu.CompilerParams(
            dimension_semantics=("arbitrary", "arbitrary"), vmem_limit_bytes=VMEM_LIMIT),
        name="experts",
    )(h2b, rows, gate, u_b, vt_b, x1, mod)


def _proj_columns():
    half = np.arange(HALF_DIM)
    cols = []
    for c in range(N_Q_HEADS // 2):
        for part in range(2):
            for head in (2 * c, 2 * c + 1):
                cols.append(head * HEAD_DIM + part * HALF_DIM + half)
    for g in range(N_KV_HEADS):
        for part in (0, 0, 1, 1):
            cols.append(ATTN_WIDTH + g * HEAD_DIM + part * HALF_DIM + half)
    for g in range(N_KV_HEADS):
        for _ in range(2):
            cols.append(ATTN_WIDTH + KV_WIDTH + g * HEAD_DIM + np.arange(HEAD_DIM))
    cols.append(np.arange(ATTN_WIDTH + 2 * KV_WIDTH, ATTN_WIDTH + 2 * KV_WIDTH + 3 * CONV_WIDTH))
    return np.concatenate(cols)


_PROJ_COLS = _proj_columns()
_TILE_DIM = np.concatenate([np.arange(HALF_DIM), np.arange(HALF_DIM),
                            HALF_DIM + np.arange(HALF_DIM), HALF_DIM + np.arange(HALF_DIM)])
_TILE_HEAD = (np.arange(LANES) // HALF_DIM) % 2
_MSAME = (_TILE_HEAD[:, None] == _TILE_HEAD[None, :]).astype(np.float32)


def kernel(x, c, positions, ada_w, ada_b, norm1_g, w_in, q_norm_g, k_norm_g, sinks, conv_w,
           attn_out_g, conv_out_g, w_out, norm2_g, peer_wq, peer_subkeys, peer_u, peer_v):
    B, S, D = x.shape
    T = B * S
    assert ada_w.shape[0] == 1, "single-layer block"
    assert D == D_MODEL and S % SEQ_TILE == 0 and T % TOK_TILE == 0 and T % ROUTER_TILE == 0
    assert S % TOK_TILE == 0

    c_pad = jnp.pad(c, ((0, SUBLANES - B), (0, 0)))
    mod = _mod_call(c_pad, ada_w[0], ada_b[0][None, :])[:B].reshape(B, N_MOD, D)

    inv_freq = ROPE_THETA ** (-jnp.arange(0, HEAD_DIM, 2, dtype=f32) / HEAD_DIM)
    freq = inv_freq[_TILE_DIM % HALF_DIM][None, :]
    sgn = jnp.asarray(np.where(np.arange(LANES) < LANES // 2, -1.0, 1.0), f32)[None, :]
    win = w_in[0][:, _PROJ_COLS].astype(bf16)
    wout = w_out[0].astype(bf16)
    gq = jnp.tile(q_norm_g[0][_TILE_DIM], ATTN_WIDTH // LANES)[None, :]
    gk = jnp.tile(k_norm_g[0][_TILE_DIM], N_KV_HEADS)[None, :]
    x1, h2t = _mixer_call(
        sinks[0], x, positions.astype(f32)[..., None], mod, norm1_g, win, gq, gk, freq, sgn,
        jnp.asarray(_MSAME, bf16), conv_w[0], attn_out_g, conv_out_g, wout, norm2_g)

    wq = peer_wq[0].T.astype(bf16)
    sk = peer_subkeys[0].reshape(2 * PEER_HEADS, PEER_KEYS, PEER_QDIM // 2).astype(bf16)
    rows, gate = _router_call(h2t, wq, sk)

    out = _experts_call(h2t, rows, gate, peer_u[0].astype(bf16), peer_v[0].T.astype(bf16),
                        x1.reshape(T, D), mod, S // TOK_TILE)
    return out.reshape(B, S, D)
```

```python
import numpy as np
import jax
import jax.numpy as jnp
from jax import lax
from jax.experimental import pallas as pl
from jax.experimental.pallas import tpu as pltpu

D_MODEL = 1024
HEAD_DIM = 64
HALF_DIM = HEAD_DIM // 2
N_Q_HEADS = 8
N_KV_HEADS = 2
ATTN_WIDTH = N_Q_HEADS * HEAD_DIM
KV_WIDTH = N_KV_HEADS * HEAD_DIM
WINDOW = 128
ROPE_THETA = 10000.0
CONV_WIDTH = D_MODEL - ATTN_WIDTH
CONV_K = 3
PEER_HEADS = 8
PEER_KEYS = 128
PEER_EXPERTS = PEER_KEYS * PEER_KEYS
PEER_TOPK = 16
PEER_QDIM = 256
N_MOD = 6
EPS = 1e-6
NEG = -1e30

LANES = 128
SUBLANES = 8
BF16_ROWS = 2 * SUBLANES
N_ROW_GROUPS = PEER_KEYS // SUBLANES
N_KEY_VREGS = PEER_KEYS // BF16_ROWS
VMEM_LIMIT = 56 * 1024 * 1024

Q_OFF = 0
K_OFF = ATTN_WIDTH
V_OFF = K_OFF + 2 * KV_WIDTH
CB_OFF = V_OFF + 2 * KV_WIDTH
CC_OFF = CB_OFF + CONV_WIDTH
CU_OFF = CC_OFF + CONV_WIDTH
PROJ_WIDTH = CU_OFF + CONV_WIDTH

SEQ_TILE = 512
ROUTER_TILE = 512
TOK_TILE = 512
EXP_TILE = 2048

_NN = (((1,), (0,)), ((), ()))
_NT = (((1,), (1,)), ((), ()))
_TN = (((0,), (0,)), ((), ()))

f32 = jnp.float32
bf16 = jnp.bfloat16


def _split(x):
    hi = x.astype(bf16)
    lo = (x - hi.astype(f32)).astype(bf16)
    return hi, lo


def _dg(a, b, dims=_NN):
    return lax.dot_general(a, b, dims, preferred_element_type=f32)


def _dot3(a, b, dims=_NN):
    return _dg(a[0], b[0], dims) + (_dg(a[0], b[1], dims) + _dg(a[1], b[0], dims))


def _rms(x):
    return x * lax.rsqrt(jnp.mean(x * x, axis=-1, keepdims=True) + EPS)


def _mod_kernel(c_ref, w_ref, b_ref, o_ref):
    c = c_ref[...]
    ca = c * (1.0 / (1.0 + jnp.exp(-c)))
    o_ref[...] = _dot3(_split(ca), _split(w_ref[...])) + b_ref[...]


def _mod_call(c_pad, ada_w, ada_b):
    nb = 1536
    n = ada_w.shape[1]
    return pl.pallas_call(
        _mod_kernel,
        grid=(n // nb,),
        in_specs=[
            pl.BlockSpec((SUBLANES, D_MODEL), lambda j: (0, 0)),
            pl.BlockSpec((D_MODEL, nb), lambda j: (0, j)),
            pl.BlockSpec((1, nb), lambda j: (0, j)),
        ],
        out_specs=pl.BlockSpec((SUBLANES, nb), lambda j: (0, j)),
        out_shape=jax.ShapeDtypeStruct((SUBLANES, n), f32),
        compiler_params=pltpu.CompilerParams(
            dimension_semantics=("arbitrary",), vmem_limit_bytes=VMEM_LIMIT),
        name="mod",
    )(c_pad, ada_w, ada_b)


def _mixer_kernel(sink_ref, x_ref, pos_ref, mod_ref, g1_ref, win_ref, gq_ref, gk_ref,
                  freq_ref, sgn_ref, msame_ref, convw_ref, ga_ref, gc_ref, wout_ref,
                  g2_ref, x1_ref, h2t_ref, qbuf, kbuf, vbuf, abuf, ucarry):
    ts = x_ref.shape[0]
    s_idx = pl.program_id(1)
    x = x_ref[...]
    mod = mod_ref[...]
    shift1, scale1, gate1 = mod[0:1], mod[1:2], mod[2:3]
    shift2, scale2 = mod[3:4], mod[4:5]

    h = _rms(x) * g1_ref[...]
    hb = (h * (1.0 + scale1) + shift1).astype(bf16)

    def proj(c0, c1):
        return _dg(hb, win_ref[:, c0:c1])

    ang = pos_ref[...] * freq_ref[...]
    cosf = jnp.cos(ang)
    sins = jnp.sin(ang) * sgn_ref[...]
    msame = msame_ref[...]

    def headnorm_rope(t, g):
        sq = _split(t * t)
        ss = _dg(sq[0], msame) + _dg(sq[1], msame)
        tn = t * lax.rsqrt(ss * (1.0 / HEAD_DIM) + EPS) * g
        return tn * cosf + pltpu.roll(tn, LANES // 2, 1) * sins

    @pl.when(s_idx == 0)
    def _():
        kbuf[0:WINDOW, :] = jnp.zeros((WINDOW, 2 * LANES), f32)
        vbuf[0:WINDOW, :] = jnp.zeros((WINDOW, 2 * LANES), f32)
        ucarry[...] = jnp.zeros(ucarry.shape, f32)

    for c in range(ATTN_WIDTH // LANES):
        sl = slice(c * LANES, (c + 1) * LANES)
        qbuf[:, sl] = headnorm_rope(proj(Q_OFF + c * LANES, Q_OFF + (c + 1) * LANES), gq_ref[:, sl])
    for g in range(N_KV_HEADS):
        sl = slice(g * LANES, (g + 1) * LANES)
        kbuf[WINDOW:WINDOW + ts, sl] = headnorm_rope(
            proj(K_OFF + g * LANES, K_OFF + (g + 1) * LANES), gk_ref[:, sl])
    vbuf[WINDOW:WINDOW + ts, :] = proj(V_OFF, V_OFF + 2 * LANES)

    qi = lax.broadcasted_iota(jnp.int32, (WINDOW, 2 * WINDOW), 0)
    kj = lax.broadcasted_iota(jnp.int32, (WINDOW, 2 * WINDOW), 1)
    band = (kj > qi) & (kj <= qi + WINDOW)
    lane = lax.broadcasted_iota(jnp.int32, (1, LANES), 1)

    def attn_block(n, carry):
        r0 = pl.multiple_of(n * WINDOW, WINDOW)
        qb = qbuf[pl.ds(r0, WINDOW), :]
        kb = kbuf[pl.ds(r0, 2 * WINDOW), :]
        vb = vbuf[pl.ds(r0, 2 * WINDOW), :]
        first = jnp.logical_and(s_idx == 0, n == 0)
        valid = band & (kj >= jnp.where(first, WINDOW, 0))
        q_per_kv = N_Q_HEADS // N_KV_HEADS
        for g in range(N_KV_HEADS):
            kt = kb[:, g * LANES:(g + 1) * LANES].astype(bf16)
            vt = vb[:, g * LANES:(g + 1) * LANES].astype(bf16)
            qs = []
            for c in (2 * g, 2 * g + 1):
                qt = qb[:, c * LANES:(c + 1) * LANES]
                for sub in range(2):
                    qs.append(jnp.where((lane // HALF_DIM) % 2 == sub, qt, 0.0).astype(bf16))
            s_all = _dg(jnp.concatenate(qs, axis=0), kt, _NT) * (HEAD_DIM ** -0.5)
            ps = []
            for hh in range(q_per_kv):
                s = jnp.where(valid, s_all[hh * WINDOW:(hh + 1) * WINDOW, :], NEG)
                sink = sink_ref[q_per_kv * g + hh]
                m = jnp.maximum(jnp.max(s, axis=-1, keepdims=True), sink)
                e = jnp.exp(s - m)
                den = jnp.sum(e, axis=-1, keepdims=True) + jnp.exp(sink - m)
                ps.append((e / den).astype(bf16))
            o_all = _dg(jnp.concatenate(ps, axis=0), vt)
            for cc in range(2):
                o = jnp.where(lane < HEAD_DIM,
                              o_all[2 * cc * WINDOW:(2 * cc + 1) * WINDOW, :],
                              o_all[(2 * cc + 1) * WINDOW:(2 * cc + 2) * WINDOW, :])
                c = 2 * g + cc
                abuf[pl.ds(r0, WINDOW), c * LANES:(c + 1) * LANES] = o
        return carry

    lax.fori_loop(0, ts // WINDOW, attn_block, 0)
    kbuf[0:WINDOW, :] = kbuf[ts:ts + WINDOW, :]
    vbuf[0:WINDOW, :] = vbuf[ts:ts + WINDOW, :]

    cb = proj(CB_OFF, CB_OFF + CONV_WIDTH)
    u = proj(CC_OFF, CC_OFF + CONV_WIDTH) * proj(CU_OFF, CU_OFF + CONV_WIDTH)
    rowi = lax.broadcasted_iota(jnp.int32, u.shape, 0)
    prev1 = ucarry[SUBLANES - 1:SUBLANES, :]
    prev2 = ucarry[SUBLANES - 2:SUBLANES - 1, :]
    u1 = jnp.where(rowi == 0, prev1, pltpu.roll(u, 1, 0))
    u2 = jnp.where(rowi == 0, prev2, jnp.where(rowi == 1, prev1, pltpu.roll(u, 2, 0)))
    ucarry[...] = u[ts - SUBLANES:ts, :]
    w = convw_ref[...]
    conv = cb * (w[0:1] * u2 + w[1:2] * u1 + w[2:3] * u)

    ra = (_rms(abuf[...]) * ga_ref[...]).astype(bf16)
    rc = (_rms(conv) * gc_ref[...]).astype(bf16)
    y = _dg(ra, wout_ref[0:ATTN_WIDTH, :]) + _dg(rc, wout_ref[ATTN_WIDTH:, :])
    x1 = x + gate1 * y
    x1_ref[...] = x1
    h2 = _rms(x1) * g2_ref[...]
    h2t_ref[...] = (h2 * (1.0 + scale2) + shift2).T.astype(bf16)


def _const_spec(shape):
    nd = len(shape)
    return pl.BlockSpec(shape, lambda *_: (0,) * nd, pipeline_mode=pl.Buffered(1))


def _mixer_call(sinks, x, pos, mod, g1, win, gq, gk, freq, sgn, msame, convw, ga, gc, wout, g2):
    B, S, D = x.shape
    ts = SEQ_TILE
    nst = S // ts
    return pl.pallas_call(
        _mixer_kernel,
        grid=(B, nst),
        in_specs=[
            pl.BlockSpec(memory_space=pltpu.SMEM),
            pl.BlockSpec((None, ts, D), lambda b, s: (b, s, 0)),
            pl.BlockSpec((None, ts, 1), lambda b, s: (b, s, 0)),
            pl.BlockSpec((None, N_MOD, D), lambda b, s: (b, 0, 0)),
            _const_spec(g1.shape), _const_spec(win.shape),
            _const_spec(gq.shape), _const_spec(gk.shape), _const_spec(freq.shape),
            _const_spec(sgn.shape), _const_spec(msame.shape), _const_spec(convw.shape),
            _const_spec(ga.shape), _const_spec(gc.shape), _const_spec(wout.shape),
            _const_spec(g2.shape),
        ],
        out_specs=[
            pl.BlockSpec((None, ts, D), lambda b, s: (b, s, 0)),
            pl.BlockSpec((D, ts), lambda b, s: (0, b * nst + s)),
        ],
        out_shape=[
            jax.ShapeDtypeStruct((B, S, D), f32),
            jax.ShapeDtypeStruct((D, B * S), bf16),
        ],
        scratch_shapes=[
            pltpu.VMEM((ts, ATTN_WIDTH), f32),
            pltpu.VMEM((WINDOW + ts, 2 * LANES), f32),
            pltpu.VMEM((WINDOW + ts, 2 * LANES), f32),
            pltpu.VMEM((ts, ATTN_WIDTH), f32),
            pltpu.VMEM((SUBLANES, CONV_WIDTH), f32),
        ],
        compiler_params=pltpu.CompilerParams(
            dimension_semantics=("arbitrary", "arbitrary"), vmem_limit_bytes=VMEM_LIMIT),
        name="mixer",
    )(sinks, x, pos, mod, g1, win, gq, gk, freq, sgn, msame, convw, ga, gc, wout, g2)


def _oddeven_merge_sort_pairs(n):
    pairs = []
    p = 1
    while p < n:
        k = p
        while k >= 1:
            for j in range(k % p, n - k, 2 * k):
                for i in range(min(k, n - j - k)):
                    if (i + j) // (2 * p) == (i + j + k) // (2 * p):
                        pairs.append((i + j, i + j + k))
            k //= 2
        p *= 2
    return pairs


_SORT16 = _oddeven_merge_sort_pairs(PEER_TOPK)


def _bitonic_desc(z):
    z = list(z)
    d = PEER_TOPK // 2
    while d >= 1:
        for r in range(PEER_TOPK):
            if not r & d:
                hi, lo = jnp.maximum(z[r], z[r + d]), jnp.minimum(z[r], z[r + d])
                z[r], z[r + d] = hi, lo
        d //= 2
    return z


def _merge_top(R, L):
    z = list(R)
    for r in range(PEER_TOPK - len(L), PEER_TOPK):
        z[r] = jnp.maximum(R[r], L[PEER_TOPK - 1 - r])
    return _bitonic_desc(z)


def _top16_rows(sc):
    x = [sc[SUBLANES * g:SUBLANES * (g + 1), :] for g in range(PEER_KEYS // SUBLANES)]
    for i, j in _SORT16:
        x[i], x[j] = jnp.maximum(x[i], x[j]), jnp.minimum(x[i], x[j])
    for shift in (4, 2, 1):
        z = [jnp.maximum(x[r], pltpu.roll(x[PEER_TOPK - 1 - r], shift, 0)) for r in range(PEER_TOPK)]
        x = _bitonic_desc(z)
    return x


def _prefix_count(test, rows):
    assert len(rows) == PEER_TOPK == 16
    sel = jnp.where
    p3 = test(rows[7])
    p2 = test(sel(p3, rows[11], rows[3]))
    p1 = test(sel(p3, sel(p2, rows[13], rows[9]), sel(p2, rows[5], rows[1])))
    hi = sel(p2, sel(p1, rows[14], rows[12]), sel(p1, rows[10], rows[8]))
    lo = sel(p2, sel(p1, rows[6], rows[4]), sel(p1, rows[2], rows[0]))
    p0 = test(sel(p3, hi, lo))
    bits = ((p3, 8.0), (p2, 4.0), (p1, 2.0), (p0, 1.0), (test(rows[15]), 1.0))
    return sum(sel(p, v, 0.0) for p, v in bits)


def _router_kernel(h2t_ref, wq_ref, sk_ref, rows_ref, gate_ref, q_s, sc_s, top_s, row_s):
    tr = h2t_ref.shape[1]
    q_s[...] = _dg(wq_ref[...], h2t_ref[...])
    top_s[...] = jnp.zeros(top_s.shape, f32)
    sub = lax.broadcasted_iota(jnp.int32, (SUBLANES, tr), 0)

    def head_body(h, carry):
        for p in range(2):
            hp = 2 * h + p
            r0 = pl.multiple_of(hp * PEER_KEYS, PEER_KEYS)
            sc = _dg(sk_ref[hp], q_s[pl.ds(r0, PEER_KEYS), :].astype(bf16))
            sc_s[hp] = sc
            top = _top16_rows(sc)
            for r in range(PEER_TOPK):
                top_s[p, r] = jnp.where(sub == h, top[r], top_s[p, r])
        return carry

    lax.fori_loop(0, PEER_HEADS, head_body, 0)

    a = [top_s[0, r] for r in range(PEER_TOPK)]
    b = [top_s[1, r] for r in range(PEER_TOPK)]
    R = [a[0] + b[r] for r in range(16)]
    R = _merge_top(R, [a[r] + b[0] for r in range(1, 16)])
    R = _merge_top(R, [a[1] + b[r] for r in range(1, 8)])
    R = _merge_top(R, [a[r] + b[1] for r in range(2, 8)])
    R = _merge_top(R, [a[2] + b[r] for r in range(2, 5)])
    R = _merge_top(R, [a[r] + b[2] for r in range(3, 5)])
    R = _merge_top(R, [a[3] + b[3]])
    tau = R[PEER_TOPK - 1]
    z = jnp.exp(R[0] - R[0])
    for r in range(1, PEER_TOPK):
        z = z + jnp.exp(R[r] - R[0])
    row_s[0] = a[0]
    row_s[1] = b[0]
    row_s[2] = 0.5 / z
    row_s[3] = tau

    def fac_body(h, carry):
        row = pl.ds(h, 1)
        s1 = sc_s[2 * h]
        s2 = sc_s[2 * h + 1]
        tau_h = row_s[3, row, :]
        b_rows = [top_s[1, r, row, :] for r in range(PEER_TOPK)]
        cnt = _prefix_count(lambda b_r: s1 + b_r >= tau_h, b_rows)
        rk = _prefix_count(lambda b_r: b_r > s2, b_rows)
        f1 = jnp.exp(s1 - row_s[0, row, :]) * row_s[2, row, :]
        f2 = jnp.exp(s2 - row_s[1, row, :])
        for tc in range(tr // LANES):
            cols = slice(tc * LANES, (tc + 1) * LANES)
            for ig in range(N_ROW_GROUPS):
                keys = slice(ig * SUBLANES, (ig + 1) * SUBLANES)
                rows_ref[tc, ig, h, 0] = f1[keys, cols]
                rows_ref[tc, ig, h, 1] = cnt[keys, cols]
            for v in range(N_KEY_VREGS):
                keys = slice(v * BF16_ROWS, (v + 1) * BF16_ROWS)
                gate_ref[tc, v, h, 0] = rk[keys, cols]
                gate_ref[tc, v, h, 1] = f2[keys, cols]
        return carry

    lax.fori_loop(0, PEER_HEADS, fac_body, 0)


def _router_call(h2t, wq, sk):
    D, T = h2t.shape
    tr = ROUTER_TILE
    rows_shape = (N_ROW_GROUPS, PEER_HEADS, 2, SUBLANES, LANES)
    gate_shape = (N_KEY_VREGS, PEER_HEADS, 2, BF16_ROWS, LANES)
    lead = lambda i: (i, 0, 0, 0, 0, 0)
    return pl.pallas_call(
        _router_kernel,
        grid=(T // tr,),
        in_specs=[
            pl.BlockSpec((D, tr), lambda i: (0, i)),
            _const_spec(wq.shape), _const_spec(sk.shape),
        ],
        out_specs=[
            pl.BlockSpec((tr // LANES,) + rows_shape, lead),
            pl.BlockSpec((tr // LANES,) + gate_shape, lead),
        ],
        out_shape=[
            jax.ShapeDtypeStruct((T // LANES,) + rows_shape, f32),
            jax.ShapeDtypeStruct((T // LANES,) + gate_shape, f32),
        ],
        scratch_shapes=[
            pltpu.VMEM((PEER_HEADS * PEER_QDIM, tr), f32),
            pltpu.VMEM((2 * PEER_HEADS, PEER_KEYS, tr), f32),
            pltpu.VMEM((2, PEER_TOPK, SUBLANES, tr), f32),
            pltpu.VMEM((4, SUBLANES, tr), f32),
        ],
        compiler_params=pltpu.CompilerParams(
            dimension_semantics=("arbitrary",), vmem_limit_bytes=VMEM_LIMIT),
        name="router",
    )(h2t, wq, sk)


_GELU_C1 = float(np.sqrt(2.0 / np.pi))
_GELU_C2 = 0.044715 * _GELU_C1


def _experts_kernel(h2b_ref, rows_ref, gate_ref, u_ref, vt_ref, x1_ref, mod_ref,
                    out_ref, acc, a_s, w_s, gate_s):
    tm = h2b_ref.shape[1]
    te = u_ref.shape[0]
    assert te % (SUBLANES * PEER_KEYS) == 0
    groups = te // (SUBLANES * PEER_KEYS)
    j = pl.program_id(1)

    @pl.when(j == 0)
    def _():
        acc[...] = jnp.zeros(acc.shape, f32)
        gate_s[...] = gate_ref[...].astype(bf16)

    a_s[:, 0:tm] = _dg(u_ref[...], h2b_ref[...])

    def packed_row(tc, h, kind, r):
        row = rows_ref[tc, j * groups + r // SUBLANES, h, kind][r % SUBLANES:r % SUBLANES + 1, :]
        return jnp.broadcast_to(row, (BF16_ROWS, LANES)).astype(bf16)

    for r in range(te // PEER_KEYS):
        for tc in range(tm // LANES):
            cols = slice(tc * LANES, (tc + 1) * LANES)
            f1 = [packed_row(tc, h, 0, r) for h in range(PEER_HEADS)]
            cnt = [packed_row(tc, h, 1, r) for h in range(PEER_HEADS)]
            for v in range(N_KEY_VREGS):
                rows = slice(r * PEER_KEYS + v * BF16_ROWS, r * PEER_KEYS + (v + 1) * BF16_ROWS)
                g = jnp.zeros((BF16_ROWS, LANES), bf16)
                for h in range(PEER_HEADS):
                    sel = jnp.where(gate_s[tc, v, h, 0] < cnt[h], gate_s[tc, v, h, 1],
                                    jnp.zeros((), bf16))
                    g = g + sel * f1[h]
                a = a_s[rows, cols]
                t = jnp.tanh(a * (_GELU_C1 + _GELU_C2 * (a * a)))
                w_s[rows, cols] = g * (a * (1.0 + t)).astype(bf16)

    acc[...] += _dg(vt_ref[...], w_s[:, 0:tm], _TN)

    @pl.when(j == pl.num_programs(1) - 1)
    def _():
        gate2 = mod_ref[N_MOD - 1:N_MOD, :]
        out_ref[...] = x1_ref[...] + gate2 * acc[...].T


def _experts_call(h2b, rows, gate, u_b, vt_b, x1, mod, tiles_per_batch):
    D, T = h2b.shape
    tm, te = TOK_TILE, EXP_TILE
    lead = lambda i, j: (i, 0, 0, 0, 0, 0)
    return pl.pallas_call(
        _experts_kernel,
        grid=(T // tm, PEER_EXPERTS // te),
        in_specs=[
            pl.BlockSpec((D, tm), lambda i, j: (0, i)),
            pl.BlockSpec((tm // LANES,) + rows.shape[1:], lead),
            pl.BlockSpec((tm // LANES,) + gate.shape[1:], lead),
            pl.BlockSpec((te, D), lambda i, j: (j, 0)),
            pl.BlockSpec((te, D), lambda i, j: (j, 0)),
            pl.BlockSpec((tm, D), lambda i, j: (i, 0)),
            pl.BlockSpec((None, N_MOD, D), lambda i, j: (i // tiles_per_batch, 0, 0)),
        ],
        out_specs=pl.BlockSpec((tm, D), lambda i, j: (i, 0)),
        out_shape=jax.ShapeDtypeStruct((T, D), f32),
        scratch_shapes=[
            pltpu.VMEM((D, tm), f32),
            pltpu.VMEM((te, tm + LANES), f32),
            pltpu.VMEM((te, tm + LANES), bf16),
            pltpu.VMEM((tm // LANES,) + gate.shape[1:], bf16),
        ],
        compiler_params=pltpu.CompilerParams(
            dimension_semantics=("arbitrary", "arbitrary"), vmem_limit_bytes=VMEM_LIMIT),
        name="experts",
    )(h2b, rows, gate, u_b, vt_b, x1, mod)


def _proj_columns():
    half = np.arange(HALF_DIM)
    cols = []
    for c in range(N_Q_HEADS // 2):
        for part in range(2):
            for head in (2 * c, 2 * c + 1):
                cols.append(head * HEAD_DIM + part * HALF_DIM + half)
    for g in range(N_KV_HEADS):
        for part in (0, 0, 1, 1):
            cols.append(ATTN_WIDTH + g * HEAD_DIM + part * HALF_DIM + half)
    for g in range(N_KV_HEADS):
        for _ in range(2):
            cols.append(ATTN_WIDTH + KV_WIDTH + g * HEAD_DIM + np.arange(HEAD_DIM))
    cols.append(np.arange(ATTN_WIDTH + 2 * KV_WIDTH, ATTN_WIDTH + 2 * KV_WIDTH + 3 * CONV_WIDTH))
    return np.concatenate(cols)


_PROJ_COLS = _proj_columns()
_TILE_DIM = np.concatenate([np.arange(HALF_DIM), np.arange(HALF_DIM),
                            HALF_DIM + np.arange(HALF_DIM), HALF_DIM + np.arange(HALF_DIM)])
_TILE_HEAD = (np.arange(LANES) // HALF_DIM) % 2
_MSAME = (_TILE_HEAD[:, None] == _TILE_HEAD[None, :]).astype(np.float32)


def kernel(x, c, positions, ada_w, ada_b, norm1_g, w_in, q_norm_g, k_norm_g, sinks, conv_w,
           attn_out_g, conv_out_g, w_out, norm2_g, peer_wq, peer_subkeys, peer_u, peer_v):
    B, S, D = x.shape
    T = B * S
    assert ada_w.shape[0] == 1, "single-layer block"
    assert D == D_MODEL and S % SEQ_TILE == 0 and T % TOK_TILE == 0 and T % ROUTER_TILE == 0
    assert S % TOK_TILE == 0

    c_pad = jnp.pad(c, ((0, SUBLANES - B), (0, 0)))
    mod = _mod_call(c_pad, ada_w[0], ada_b[0][None, :])[:B].reshape(B, N_MOD, D)

    inv_freq = ROPE_THETA ** (-jnp.arange(0, HEAD_DIM, 2, dtype=f32) / HEAD_DIM)
    freq = inv_freq[_TILE_DIM % HALF_DIM][None, :]
    sgn = jnp.asarray(np.where(np.arange(LANES) < LANES // 2, -1.0, 1.0), f32)[None, :]
    win = w_in[0][:, _PROJ_COLS].astype(bf16)
    wout = w_out[0].astype(bf16)
    gq = jnp.tile(q_norm_g[0][_TILE_DIM], ATTN_WIDTH // LANES)[None, :]
    gk = jnp.tile(k_norm_g[0][_TILE_DIM], N_KV_HEADS)[None, :]
    x1, h2t = _mixer_call(
        sinks[0], x, positions.astype(f32)[..., None], mod, norm1_g, win, gq, gk, freq, sgn,
        jnp.asarray(_MSAME, bf16), conv_w[0], attn_out_g, conv_out_g, wout, norm2_g)

    wq = peer_wq[0].T.astype(bf16)
    sk = peer_subkeys[0].reshape(2 * PEER_HEADS, PEER_KEYS, PEER_QDIM // 2).astype(bf16)
    rows, gate = _router_call(h2t, wq, sk)

    out = _experts_call(h2t, rows, gate, peer_u[0].astype(bf16), peer_v[0].astype(bf16),
                        x1.reshape(T, D), mod, S // TOK_TILE)
    return out.reshape(B, S, D)
```

```python
import numpy as np
import jax
import jax.numpy as jnp
from jax import lax
from jax.experimental import pallas as pl
from jax.experimental.pallas import tpu as pltpu

D_MODEL = 1024
HEAD_DIM = 64
HALF_DIM = HEAD_DIM // 2
N_Q_HEADS = 8
N_KV_HEADS = 2
ATTN_WIDTH = N_Q_HEADS * HEAD_DIM
KV_WIDTH = N_KV_HEADS * HEAD_DIM
WINDOW = 128
ROPE_THETA = 10000.0
CONV_WIDTH = D_MODEL - ATTN_WIDTH
CONV_K = 3
PEER_HEADS = 8
PEER_KEYS = 128
PEER_EXPERTS = PEER_KEYS * PEER_KEYS
PEER_TOPK = 16
PEER_QDIM = 256
N_MOD = 6
EPS = 1e-6
NEG = -1e30

LANES = 128
SUBLANES = 8
BF16_ROWS = 2 * SUBLANES
N_ROW_GROUPS = PEER_KEYS // SUBLANES
N_KEY_VREGS = PEER_KEYS // BF16_ROWS
VMEM_LIMIT = 56 * 1024 * 1024

Q_OFF = 0
K_OFF = ATTN_WIDTH
V_OFF = K_OFF + 2 * KV_WIDTH
CB_OFF = V_OFF + 2 * KV_WIDTH
CC_OFF = CB_OFF + CONV_WIDTH
CU_OFF = CC_OFF + CONV_WIDTH
PROJ_WIDTH = CU_OFF + CONV_WIDTH

SEQ_TILE = 512
ROUTER_TILE = 512
TOK_TILE = 512
EXP_TILE = 2048

_NN = (((1,), (0,)), ((), ()))
_NT = (((1,), (1,)), ((), ()))
_TN = (((0,), (0,)), ((), ()))

f32 = jnp.float32
bf16 = jnp.bfloat16


def _split(x):
    hi = x.astype(bf16)
    lo = (x - hi.astype(f32)).astype(bf16)
    return hi, lo


def _dg(a, b, dims=_NN):
    return lax.dot_general(a, b, dims, preferred_element_type=f32)


def _dot3(a, b, dims=_NN):
    return _dg(a[0], b[0], dims) + (_dg(a[0], b[1], dims) + _dg(a[1], b[0], dims))


def _rms(x):
    return x * lax.rsqrt(jnp.mean(x * x, axis=-1, keepdims=True) + EPS)


def _mod_kernel(c_ref, w_ref, b_ref, o_ref):
    c = c_ref[...]
    ca = c * (1.0 / (1.0 + jnp.exp(-c)))
    o_ref[...] = _dot3(_split(ca), _split(w_ref[...])) + b_ref[...]


def _mod_call(c_pad, ada_w, ada_b):
    nb = 1536
    n = ada_w.shape[1]
    return pl.pallas_call(
        _mod_kernel,
        grid=(n // nb,),
        in_specs=[
            pl.BlockSpec((SUBLANES, D_MODEL), lambda j: (0, 0)),
            pl.BlockSpec((D_MODEL, nb), lambda j: (0, j)),
            pl.BlockSpec((1, nb), lambda j: (0, j)),
        ],
        out_specs=pl.BlockSpec((SUBLANES, nb), lambda j: (0, j)),
        out_shape=jax.ShapeDtypeStruct((SUBLANES, n), f32),
        compiler_params=pltpu.CompilerParams(
            dimension_semantics=("arbitrary",), vmem_limit_bytes=VMEM_LIMIT),
        name="mod",
    )(c_pad, ada_w, ada_b)


def _mixer_kernel(sink_ref, x_ref, pos_ref, mod_ref, g1_ref, win_ref, gq_ref, gk_ref,
                  freq_ref, sgn_ref, msame_ref, convw_ref, ga_ref, gc_ref, wout_ref,
                  g2_ref, x1_ref, h2t_ref, qbuf, kbuf, vbuf, abuf, ucarry):
    ts = x_ref.shape[0]
    s_idx = pl.program_id(1)
    x = x_ref[...]
    mod = mod_ref[...]
    shift1, scale1, gate1 = mod[0:1], mod[1:2], mod[2:3]
    shift2, scale2 = mod[3:4], mod[4:5]

    h = _rms(x) * g1_ref[...]
    hb = (h * (1.0 + scale1) + shift1).astype(bf16)

    def proj(c0, c1):
        return _dg(hb, win_ref[:, c0:c1])

    ang = pos_ref[...] * freq_ref[...]
    cosf = jnp.cos(ang)
    sins = jnp.sin(ang) * sgn_ref[...]
    msame = msame_ref[...]

    def headnorm_rope(t, g):
        sq = _split(t * t)
        ss = _dg(sq[0], msame) + _dg(sq[1], msame)
        tn = t * lax.rsqrt(ss * (1.0 / HEAD_DIM) + EPS) * g
        return tn * cosf + pltpu.roll(tn, LANES // 2, 1) * sins

    @pl.when(s_idx == 0)
    def _():
        kbuf[0:WINDOW, :] = jnp.zeros((WINDOW, 2 * LANES), f32)
        vbuf[0:WINDOW, :] = jnp.zeros((WINDOW, 2 * LANES), f32)
        ucarry[...] = jnp.zeros(ucarry.shape, f32)

    for c in range(ATTN_WIDTH // LANES):
        sl = slice(c * LANES, (c + 1) * LANES)
        qbuf[:, sl] = headnorm_rope(proj(Q_OFF + c * LANES, Q_OFF + (c + 1) * LANES), gq_ref[:, sl])
    for g in range(N_KV_HEADS):
        sl = slice(g * LANES, (g + 1) * LANES)
        kbuf[WINDOW:WINDOW + ts, sl] = headnorm_rope(
            proj(K_OFF + g * LANES, K_OFF + (g + 1) * LANES), gk_ref[:, sl])
    vbuf[WINDOW:WINDOW + ts, :] = proj(V_OFF, V_OFF + 2 * LANES)

    qi = lax.broadcasted_iota(jnp.int32, (WINDOW, 2 * WINDOW), 0)
    kj = lax.broadcasted_iota(jnp.int32, (WINDOW, 2 * WINDOW), 1)
    band = (kj > qi) & (kj <= qi + WINDOW)
    lane = lax.broadcasted_iota(jnp.int32, (1, LANES), 1)

    def attn_block(n, carry):
        r0 = pl.multiple_of(n * WINDOW, WINDOW)
        qb = qbuf[pl.ds(r0, WINDOW), :]
        kb = kbuf[pl.ds(r0, 2 * WINDOW), :]
        vb = vbuf[pl.ds(r0, 2 * WINDOW), :]
        first = jnp.logical_and(s_idx == 0, n == 0)
        valid = band & (kj >= jnp.where(first, WINDOW, 0))
        q_per_kv = N_Q_HEADS // N_KV_HEADS
        for g in range(N_KV_HEADS):
            kt = kb[:, g * LANES:(g + 1) * LANES].astype(bf16)
            vt = vb[:, g * LANES:(g + 1) * LANES].astype(bf16)
            qs = []
            for c in (2 * g, 2 * g + 1):
                qt = qb[:, c * LANES:(c + 1) * LANES]
                for sub in range(2):
                    qs.append(jnp.where((lane // HALF_DIM) % 2 == sub, qt, 0.0).astype(bf16))
            s_all = _dg(jnp.concatenate(qs, axis=0), kt, _NT) * (HEAD_DIM ** -0.5)
            ps = []
            for hh in range(q_per_kv):
                s = jnp.where(valid, s_all[hh * WINDOW:(hh + 1) * WINDOW, :], NEG)
                sink = sink_ref[q_per_kv * g + hh]
                m = jnp.maximum(jnp.max(s, axis=-1, keepdims=True), sink)
                e = jnp.exp(s - m)
                den = jnp.sum(e, axis=-1, keepdims=True) + jnp.exp(sink - m)
                ps.append((e / den).astype(bf16))
            o_all = _dg(jnp.concatenate(ps, axis=0), vt)
            for cc in range(2):
                o = jnp.where(lane < HEAD_DIM,
                              o_all[2 * cc * WINDOW:(2 * cc + 1) * WINDOW, :],
                              o_all[(2 * cc + 1) * WINDOW:(2 * cc + 2) * WINDOW, :])
                c = 2 * g + cc
                abuf[pl.ds(r0, WINDOW), c * LANES:(c + 1) * LANES] = o
        return carry

    lax.fori_loop(0, ts // WINDOW, attn_block, 0, unroll=True)
    kbuf[0:WINDOW, :] = kbuf[ts:ts + WINDOW, :]
    vbuf[0:WINDOW, :] = vbuf[ts:ts + WINDOW, :]

    cb = proj(CB_OFF, CB_OFF + CONV_WIDTH)
    u = proj(CC_OFF, CC_OFF + CONV_WIDTH) * proj(CU_OFF, CU_OFF + CONV_WIDTH)
    rowi = lax.broadcasted_iota(jnp.int32, u.shape, 0)
    prev1 = ucarry[SUBLANES - 1:SUBLANES, :]
    prev2 = ucarry[SUBLANES - 2:SUBLANES - 1, :]
    u1 = jnp.where(rowi == 0, prev1, pltpu.roll(u, 1, 0))
    u2 = jnp.where(rowi == 0, prev2, jnp.where(rowi == 1, prev1, pltpu.roll(u, 2, 0)))
    ucarry[...] = u[ts - SUBLANES:ts, :]
    w = convw_ref[...]
    conv = cb * (w[0:1] * u2 + w[1:2] * u1 + w[2:3] * u)

    ra = (_rms(abuf[...]) * ga_ref[...]).astype(bf16)
    rc = (_rms(conv) * gc_ref[...]).astype(bf16)
    y = _dg(ra, wout_ref[0:ATTN_WIDTH, :]) + _dg(rc, wout_ref[ATTN_WIDTH:, :])
    x1 = x + gate1 * y
    x1_ref[...] = x1
    h2 = _rms(x1) * g2_ref[...]
    h2t_ref[...] = (h2 * (1.0 + scale2) + shift2).T.astype(bf16)


def _const_spec(shape):
    nd = len(shape)
    return pl.BlockSpec(shape, lambda *_: (0,) * nd, pipeline_mode=pl.Buffered(1))


def _mixer_call(sinks, x, pos, mod, g1, win, gq, gk, freq, sgn, msame, convw, ga, gc, wout, g2):
    B, S, D = x.shape
    ts = SEQ_TILE
    nst = S // ts
    return pl.pallas_call(
        _mixer_kernel,
        grid=(B, nst),
        in_specs=[
            pl.BlockSpec(memory_space=pltpu.SMEM),
            pl.BlockSpec((None, ts, D), lambda b, s: (b, s, 0)),
            pl.BlockSpec((None, ts, 1), lambda b, s: (b, s, 0)),
            pl.BlockSpec((None, N_MOD, D), lambda b, s: (b, 0, 0)),
            _const_spec(g1.shape), _const_spec(win.shape),
            _const_spec(gq.shape), _const_spec(gk.shape), _const_spec(freq.shape),
            _const_spec(sgn.shape), _const_spec(msame.shape), _const_spec(convw.shape),
            _const_spec(ga.shape), _const_spec(gc.shape), _const_spec(wout.shape),
            _const_spec(g2.shape),
        ],
        out_specs=[
            pl.BlockSpec((None, ts, D), lambda b, s: (b, s, 0)),
            pl.BlockSpec((D, ts), lambda b, s: (0, b * nst + s)),
        ],
        out_shape=[
            jax.ShapeDtypeStruct((B, S, D), f32),
            jax.ShapeDtypeStruct((D, B * S), bf16),
        ],
        scratch_shapes=[
            pltpu.VMEM((ts, ATTN_WIDTH), f32),
            pltpu.VMEM((WINDOW + ts, 2 * LANES), f32),
            pltpu.VMEM((WINDOW + ts, 2 * LANES), f32),
            pltpu.VMEM((ts, ATTN_WIDTH), f32),
            pltpu.VMEM((SUBLANES, CONV_WIDTH), f32),
        ],
        compiler_params=pltpu.CompilerParams(
            dimension_semantics=("arbitrary", "arbitrary"), vmem_limit_bytes=VMEM_LIMIT),
        name="mixer",
    )(sinks, x, pos, mod, g1, win, gq, gk, freq, sgn, msame, convw, ga, gc, wout, g2)


def _oddeven_merge_sort_pairs(n):
    pairs = []
    p = 1
    while p < n:
        k = p
        while k >= 1:
            for j in range(k % p, n - k, 2 * k):
                for i in range(min(k, n - j - k)):
                    if (i + j) // (2 * p) == (i + j + k) // (2 * p):
                        pairs.append((i + j, i + j + k))
            k //= 2
        p *= 2
    return pairs


_SORT16 = _oddeven_merge_sort_pairs(PEER_TOPK)


def _bitonic_desc(z):
    z = list(z)
    d = PEER_TOPK // 2
    while d >= 1:
        for r in range(PEER_TOPK):
            if not r & d:
                hi, lo = jnp.maximum(z[r], z[r + d]), jnp.minimum(z[r], z[r + d])
                z[r], z[r + d] = hi, lo
        d //= 2
    return z


def _merge_top(R, L):
    z = list(R)
    for r in range(PEER_TOPK - len(L), PEER_TOPK):
        z[r] = jnp.maximum(R[r], L[PEER_TOPK - 1 - r])
    return _bitonic_desc(z)


def _top16_rows(sc):
    x = [sc[SUBLANES * g:SUBLANES * (g + 1), :] for g in range(PEER_KEYS // SUBLANES)]
    for i, j in _SORT16:
        x[i], x[j] = jnp.maximum(x[i], x[j]), jnp.minimum(x[i], x[j])
    for shift in (4, 2, 1):
        z = [jnp.maximum(x[r], pltpu.roll(x[PEER_TOPK - 1 - r], shift, 0)) for r in range(PEER_TOPK)]
        x = _bitonic_desc(z)
    return x


def _prefix_count(test, rows):
    assert len(rows) == PEER_TOPK == 16
    sel = jnp.where
    p3 = test(rows[7])
    p2 = test(sel(p3, rows[11], rows[3]))
    p1 = test(sel(p3, sel(p2, rows[13], rows[9]), sel(p2, rows[5], rows[1])))
    hi = sel(p2, sel(p1, rows[14], rows[12]), sel(p1, rows[10], rows[8]))
    lo = sel(p2, sel(p1, rows[6], rows[4]), sel(p1, rows[2], rows[0]))
    p0 = test(sel(p3, hi, lo))
    bits = ((p3, 8.0), (p2, 4.0), (p1, 2.0), (p0, 1.0), (test(rows[15]), 1.0))
    return sum(sel(p, v, 0.0) for p, v in bits)


def _router_kernel(h2t_ref, wq_ref, sk_ref, rows_ref, gate_ref, q_s, sc_s, top_s, row_s):
    tr = h2t_ref.shape[1]
    q_s[...] = _dg(wq_ref[...], h2t_ref[...])
    top_s[...] = jnp.zeros(top_s.shape, f32)
    sub = lax.broadcasted_iota(jnp.int32, (SUBLANES, tr), 0)

    def head_body(h, carry):
        for p in range(2):
            hp = 2 * h + p
            r0 = pl.multiple_of(hp * PEER_KEYS, PEER_KEYS)
            sc = _dg(sk_ref[hp], q_s[pl.ds(r0, PEER_KEYS), :].astype(bf16))
            sc_s[hp] = sc
            top = _top16_rows(sc)
            for r in range(PEER_TOPK):
                top_s[p, r] = jnp.where(sub == h, top[r], top_s[p, r])
        return carry

    lax.fori_loop(0, PEER_HEADS, head_body, 0)

    a = [top_s[0, r] for r in range(PEER_TOPK)]
    b = [top_s[1, r] for r in range(PEER_TOPK)]
    R = [a[0] + b[r] for r in range(16)]
    R = _merge_top(R, [a[r] + b[0] for r in range(1, 16)])
    R = _merge_top(R, [a[1] + b[r] for r in range(1, 8)])
    R = _merge_top(R, [a[r] + b[1] for r in range(2, 8)])
    R = _merge_top(R, [a[2] + b[r] for r in range(2, 5)])
    R = _merge_top(R, [a[r] + b[2] for r in range(3, 5)])
    R = _merge_top(R, [a[3] + b[3]])
    tau = R[PEER_TOPK - 1]
    z = jnp.exp(R[0] - R[0])
    for r in range(1, PEER_TOPK):
        z = z + jnp.exp(R[r] - R[0])
    row_s[0] = a[0]
    row_s[1] = b[0]
    row_s[2] = 0.5 / z
    row_s[3] = tau

    def fac_body(h, carry):
        row = pl.ds(h, 1)
        s1 = sc_s[2 * h]
        s2 = sc_s[2 * h + 1]
        tau_h = row_s[3, row, :]
        b_rows = [top_s[1, r, row, :] for r in range(PEER_TOPK)]
        cnt = _prefix_count(lambda b_r: s1 + b_r >= tau_h, b_rows)
        rk = _prefix_count(lambda b_r: b_r > s2, b_rows)
        f1 = jnp.exp(s1 - row_s[0, row, :]) * row_s[2, row, :]
        f2 = jnp.exp(s2 - row_s[1, row, :])
        for tc in range(tr // LANES):
            cols = slice(tc * LANES, (tc + 1) * LANES)
            for ig in range(N_ROW_GROUPS):
                keys = slice(ig * SUBLANES, (ig + 1) * SUBLANES)
                rows_ref[tc, ig, h, 0] = f1[keys, cols]
                rows_ref[tc, ig, h, 1] = cnt[keys, cols]
            for v in range(N_KEY_VREGS):
                keys = slice(v * BF16_ROWS, (v + 1) * BF16_ROWS)
                gate_ref[tc, v, h, 0] = rk[keys, cols]
                gate_ref[tc, v, h, 1] = f2[keys, cols]
        return carry

    lax.fori_loop(0, PEER_HEADS, fac_body, 0)


def _router_call(h2t, wq, sk):
    D, T = h2t.shape
    tr = ROUTER_TILE
    rows_shape = (N_ROW_GROUPS, PEER_HEADS, 2, SUBLANES, LANES)
    gate_shape = (N_KEY_VREGS, PEER_HEADS, 2, BF16_ROWS, LANES)
    lead = lambda i: (i, 0, 0, 0, 0, 0)
    return pl.pallas_call(
        _router_kernel,
        grid=(T // tr,),
        in_specs=[
            pl.BlockSpec((D, tr), lambda i: (0, i)),
            _const_spec(wq.shape), _const_spec(sk.shape),
        ],
        out_specs=[
            pl.BlockSpec((tr // LANES,) + rows_shape, lead),
            pl.BlockSpec((tr // LANES,) + gate_shape, lead),
        ],
        out_shape=[
            jax.ShapeDtypeStruct((T // LANES,) + rows_shape, f32),
            jax.ShapeDtypeStruct((T // LANES,) + gate_shape, f32),
        ],
        scratch_shapes=[
            pltpu.VMEM((PEER_HEADS * PEER_QDIM, tr), f32),
            pltpu.VMEM((2 * PEER_HEADS, PEER_KEYS, tr), f32),
            pltpu.VMEM((2, PEER_TOPK, SUBLANES, tr), f32),
            pltpu.VMEM((4, SUBLANES, tr), f32),
        ],
        compiler_params=pltpu.CompilerParams(
            dimension_semantics=("arbitrary",), vmem_limit_bytes=VMEM_LIMIT),
        name="router",
    )(h2t, wq, sk)


_GELU_C1 = float(np.sqrt(2.0 / np.pi))
_GELU_C2 = 0.044715 * _GELU_C1


def _experts_kernel(h2b_ref, rows_ref, gate_ref, u_ref, vt_ref, x1_ref, mod_ref,
                    out_ref, acc, a_s, w_s, gate_s):
    tm = h2b_ref.shape[1]
    te = u_ref.shape[0]
    assert te % (SUBLANES * PEER_KEYS) == 0
    groups = te // (SUBLANES * PEER_KEYS)
    j = pl.program_id(1)

    @pl.when(j == 0)
    def _():
        acc[...] = jnp.zeros(acc.shape, f32)
        gate_s[...] = gate_ref[...].astype(bf16)

    a_s[:, 0:tm] = _dg(u_ref[...], h2b_ref[...])

    def packed_row(tc, h, kind, r):
        row = rows_ref[tc, j * groups + r // SUBLANES, h, kind][r % SUBLANES:r % SUBLANES + 1, :]
        return jnp.broadcast_to(row, (BF16_ROWS, LANES)).astype(bf16)

    for r in range(te // PEER_KEYS):
        for tc in range(tm // LANES):
            cols = slice(tc * LANES, (tc + 1) * LANES)
            f1 = [packed_row(tc, h, 0, r) for h in range(PEER_HEADS)]
            cnt = [packed_row(tc, h, 1, r) for h in range(PEER_HEADS)]
            for v in range(N_KEY_VREGS):
                rows = slice(r * PEER_KEYS + v * BF16_ROWS, r * PEER_KEYS + (v + 1) * BF16_ROWS)
                g = jnp.zeros((BF16_ROWS, LANES), bf16)
                for h in range(PEER_HEADS):
                    sel = jnp.where(gate_s[tc, v, h, 0] < cnt[h], gate_s[tc, v, h, 1],
                                    jnp.zeros((), bf16))
                    g = g + sel * f1[h]
                a = a_s[rows, cols]
                t = jnp.tanh(a * (_GELU_C1 + _GELU_C2 * (a * a)))
                w_s[rows, cols] = g * (a * (1.0 + t)).astype(bf16)

    acc[...] += _dg(vt_ref[...], w_s[:, 0:tm], _TN)

    @pl.when(j == pl.num_programs(1) - 1)
    def _():
        gate2 = mod_ref[N_MOD - 1:N_MOD, :]
        out_ref[...] = x1_ref[...] + gate2 * acc[...].T


def _experts_call(h2b, rows, gate, u_b, vt_b, x1, mod, tiles_per_batch):
    D, T = h2b.shape
    tm, te = TOK_TILE, EXP_TILE
    lead = lambda i, j: (i, 0, 0, 0, 0, 0)
    return pl.pallas_call(
        _experts_kernel,
        grid=(T // tm, PEER_EXPERTS // te),
        in_specs=[
            pl.BlockSpec((D, tm), lambda i, j: (0, i)),
            pl.BlockSpec((tm // LANES,) + rows.shape[1:], lead),
            pl.BlockSpec((tm // LANES,) + gate.shape[1:], lead),
            pl.BlockSpec((te, D), lambda i, j: (j, 0)),
            pl.BlockSpec((te, D), lambda i, j: (j, 0)),
            pl.BlockSpec((tm, D), lambda i, j: (i, 0)),
            pl.BlockSpec((None, N_MOD, D), lambda i, j: (i // tiles_per_batch, 0, 0)),
        ],
        out_specs=pl.BlockSpec((tm, D), lambda i, j: (i, 0)),
        out_shape=jax.ShapeDtypeStruct((T, D), f32),
        scratch_shapes=[
            pltpu.VMEM((D, tm), f32),
            pltpu.VMEM((te, tm + LANES), f32),
            pltpu.VMEM((te, tm + LANES), bf16),
            pltpu.VMEM((tm // LANES,) + gate.shape[1:], bf16),
        ],
        compiler_params=pltpu.CompilerParams(
            dimension_semantics=("arbitrary", "arbitrary"), vmem_limit_bytes=VMEM_LIMIT),
        name="experts",
    )(h2b, rows, gate, u_b, vt_b, x1, mod)


def _proj_columns():
    half = np.arange(HALF_DIM)
    cols = []
    for c in range(N_Q_HEADS // 2):
        for part in range(2):
            for head in (2 * c, 2 * c + 1):
                cols.append(head * HEAD_DIM + part * HALF_DIM + half)
    for g in range(N_KV_HEADS):
        for part in (0, 0, 1, 1):
            cols.append(ATTN_WIDTH + g * HEAD_DIM + part * HALF_DIM + half)
    for g in range(N_KV_HEADS):
        for _ in range(2):
            cols.append(ATTN_WIDTH + KV_WIDTH + g * HEAD_DIM + np.arange(HEAD_DIM))
    cols.append(np.arange(ATTN_WIDTH + 2 * KV_WIDTH, ATTN_WIDTH + 2 * KV_WIDTH + 3 * CONV_WIDTH))
    return np.concatenate(cols)


_PROJ_COLS = _proj_columns()
_TILE_DIM = np.concatenate([np.arange(HALF_DIM), np.arange(HALF_DIM),
                            HALF_DIM + np.arange(HALF_DIM), HALF_DIM + np.arange(HALF_DIM)])
_TILE_HEAD = (np.arange(LANES) // HALF_DIM) % 2
_MSAME = (_TILE_HEAD[:, None] == _TILE_HEAD[None, :]).astype(np.float32)


def kernel(x, c, positions, ada_w, ada_b, norm1_g, w_in, q_norm_g, k_norm_g, sinks, conv_w,
           attn_out_g, conv_out_g, w_out, norm2_g, peer_wq, peer_subkeys, peer_u, peer_v):
    B, S, D = x.shape
    T = B * S
    assert ada_w.shape[0] == 1, "single-layer block"
    assert D == D_MODEL and S % SEQ_TILE == 0 and T % TOK_TILE == 0 and T % ROUTER_TILE == 0
    assert S % TOK_TILE == 0

    c_pad = jnp.pad(c, ((0, SUBLANES - B), (0, 0)))
    mod = _mod_call(c_pad, ada_w[0], ada_b[0][None, :])[:B].reshape(B, N_MOD, D)

    inv_freq = ROPE_THETA ** (-jnp.arange(0, HEAD_DIM, 2, dtype=f32) / HEAD_DIM)
    freq = inv_freq[_TILE_DIM % HALF_DIM][None, :]
    sgn = jnp.asarray(np.where(np.arange(LANES) < LANES // 2, -1.0, 1.0), f32)[None, :]
    win = w_in[0][:, _PROJ_COLS].astype(bf16)
    wout = w_out[0].astype(bf16)
    gq = jnp.tile(q_norm_g[0][_TILE_DIM], ATTN_WIDTH // LANES)[None, :]
    gk = jnp.tile(k_norm_g[0][_TILE_DIM], N_KV_HEADS)[None, :]
    x1, h2t = _mixer_call(
        sinks[0], x, positions.astype(f32)[..., None], mod, norm1_g, win, gq, gk, freq, sgn,
        jnp.asarray(_MSAME, bf16), conv_w[0], attn_out_g, conv_out_g, wout, norm2_g)

    wq = peer_wq[0].T.astype(bf16)
    sk = peer_subkeys[0].reshape(2 * PEER_HEADS, PEER_KEYS, PEER_QDIM // 2).astype(bf16)
    rows, gate = _router_call(h2t, wq, sk)

    out = _experts_call(h2t, rows, gate, peer_u[0].astype(bf16), peer_v[0].astype(bf16),
                        x1.reshape(T, D), mod, S // TOK_TILE)
    return out.reshape(B, S, D)
```

```python
import numpy as np
import jax
import jax.numpy as jnp
from jax import lax
from jax.experimental import pallas as pl
from jax.experimental.pallas import tpu as pltpu

D_MODEL = 1024
HEAD_DIM = 64
HALF_DIM = HEAD_DIM // 2
N_Q_HEADS = 8
N_KV_HEADS = 2
ATTN_WIDTH = N_Q_HEADS * HEAD_DIM
KV_WIDTH = N_KV_HEADS * HEAD_DIM
WINDOW = 128
ROPE_THETA = 10000.0
CONV_WIDTH = D_MODEL - ATTN_WIDTH
CONV_K = 3
PEER_HEADS = 8
PEER_KEYS = 128
PEER_EXPERTS = PEER_KEYS * PEER_KEYS
PEER_TOPK = 16
PEER_QDIM = 256
N_MOD = 6
EPS = 1e-6
NEG = -1e30

LANES = 128
SUBLANES = 8
BF16_ROWS = 2 * SUBLANES
N_ROW_GROUPS = PEER_KEYS // SUBLANES
N_KEY_VREGS = PEER_KEYS // BF16_ROWS
VMEM_LIMIT = 56 * 1024 * 1024

Q_OFF = 0
K_OFF = ATTN_WIDTH
V_OFF = K_OFF + 2 * KV_WIDTH
CB_OFF = V_OFF + 2 * KV_WIDTH
CC_OFF = CB_OFF + CONV_WIDTH
CU_OFF = CC_OFF + CONV_WIDTH
PROJ_WIDTH = CU_OFF + CONV_WIDTH

SEQ_TILE = 512
ROUTER_TILE = 512
TOK_TILE = 512
EXP_TILE = 2048

_NN = (((1,), (0,)), ((), ()))
_NT = (((1,), (1,)), ((), ()))
_TN = (((0,), (0,)), ((), ()))

f32 = jnp.float32
bf16 = jnp.bfloat16


def _split(x):
    hi = x.astype(bf16)
    lo = (x - hi.astype(f32)).astype(bf16)
    return hi, lo


def _dg(a, b, dims=_NN):
    return lax.dot_general(a, b, dims, preferred_element_type=f32)


def _dot3(a, b, dims=_NN):
    return _dg(a[0], b[0], dims) + (_dg(a[0], b[1], dims) + _dg(a[1], b[0], dims))


def _rms(x):
    return x * lax.rsqrt(jnp.mean(x * x, axis=-1, keepdims=True) + EPS)


def _mod_kernel(c_ref, w_ref, b_ref, o_ref):
    c = c_ref[...]
    ca = c * (1.0 / (1.0 + jnp.exp(-c)))
    o_ref[...] = _dot3(_split(ca), _split(w_ref[...])) + b_ref[...]


def _mod_call(c_pad, ada_w, ada_b):
    nb = 1536
    n = ada_w.shape[1]
    return pl.pallas_call(
        _mod_kernel,
        grid=(n // nb,),
        in_specs=[
            pl.BlockSpec((SUBLANES, D_MODEL), lambda j: (0, 0)),
            pl.BlockSpec((D_MODEL, nb), lambda j: (0, j)),
            pl.BlockSpec((1, nb), lambda j: (0, j)),
        ],
        out_specs=pl.BlockSpec((SUBLANES, nb), lambda j: (0, j)),
        out_shape=jax.ShapeDtypeStruct((SUBLANES, n), f32),
        compiler_params=pltpu.CompilerParams(
            dimension_semantics=("arbitrary",), vmem_limit_bytes=VMEM_LIMIT),
        name="mod",
    )(c_pad, ada_w, ada_b)


def _mixer_kernel(sink_ref, x_ref, pos_ref, mod_ref, g1_ref, win_ref, gq_ref, gk_ref,
                  freq_ref, sgn_ref, msame_ref, convw_ref, ga_ref, gc_ref, wout_ref,
                  g2_ref, x1_ref, h2t_ref, qbuf, kbuf, vbuf, abuf, ucarry):
    ts = x_ref.shape[0]
    s_idx = pl.program_id(1)
    x = x_ref[...]
    mod = mod_ref[...]
    shift1, scale1, gate1 = mod[0:1], mod[1:2], mod[2:3]
    shift2, scale2 = mod[3:4], mod[4:5]

    h = _rms(x) * g1_ref[...]
    hb = (h * (1.0 + scale1) + shift1).astype(bf16)

    def proj(c0, c1):
        return _dg(hb, win_ref[:, c0:c1])

    ang = pos_ref[...] * freq_ref[...]
    cosf = jnp.cos(ang)
    sins = jnp.sin(ang) * sgn_ref[...]
    msame = msame_ref[...]

    def headnorm_rope(t, g):
        sq = _split(t * t)
        ss = _dg(sq[0], msame) + _dg(sq[1], msame)
        tn = t * lax.rsqrt(ss * (1.0 / HEAD_DIM) + EPS) * g
        return tn * cosf + pltpu.roll(tn, LANES // 2, 1) * sins

    @pl.when(s_idx == 0)
    def _():
        kbuf[0:WINDOW, :] = jnp.zeros((WINDOW, 2 * LANES), f32)
        vbuf[0:WINDOW, :] = jnp.zeros((WINDOW, 2 * LANES), f32)
        ucarry[...] = jnp.zeros(ucarry.shape, f32)

    for c in range(ATTN_WIDTH // LANES):
        sl = slice(c * LANES, (c + 1) * LANES)
        qbuf[:, sl] = headnorm_rope(proj(Q_OFF + c * LANES, Q_OFF + (c + 1) * LANES), gq_ref[:, sl])
    for g in range(N_KV_HEADS):
        sl = slice(g * LANES, (g + 1) * LANES)
        kbuf[WINDOW:WINDOW + ts, sl] = headnorm_rope(
            proj(K_OFF + g * LANES, K_OFF + (g + 1) * LANES), gk_ref[:, sl])
    vbuf[WINDOW:WINDOW + ts, :] = proj(V_OFF, V_OFF + 2 * LANES)

    qi = lax.broadcasted_iota(jnp.int32, (WINDOW, 2 * WINDOW), 0)
    kj = lax.broadcasted_iota(jnp.int32, (WINDOW, 2 * WINDOW), 1)
    band = (kj > qi) & (kj <= qi + WINDOW)
    lane = lax.broadcasted_iota(jnp.int32, (1, LANES), 1)

    def attn_block(n, carry):
        r0 = pl.multiple_of(n * WINDOW, WINDOW)
        qb = qbuf[pl.ds(r0, WINDOW), :]
        kb = kbuf[pl.ds(r0, 2 * WINDOW), :]
        vb = vbuf[pl.ds(r0, 2 * WINDOW), :]
        first = jnp.logical_and(s_idx == 0, n == 0)
        valid = band & (kj >= jnp.where(first, WINDOW, 0))
        q_per_kv = N_Q_HEADS // N_KV_HEADS
        for g in range(N_KV_HEADS):
            kt = kb[:, g * LANES:(g + 1) * LANES].astype(bf16)
            vt = vb[:, g * LANES:(g + 1) * LANES].astype(bf16)
            qs = []
            for c in (2 * g, 2 * g + 1):
                qt = qb[:, c * LANES:(c + 1) * LANES]
                for sub in range(2):
                    qs.append(jnp.where((lane // HALF_DIM) % 2 == sub, qt, 0.0).astype(bf16))
            s_all = _dg(jnp.concatenate(qs, axis=0), kt, _NT) * (HEAD_DIM ** -0.5)
            ps = []
            for hh in range(q_per_kv):
                s = jnp.where(valid, s_all[hh * WINDOW:(hh + 1) * WINDOW, :], NEG)
                sink = sink_ref[q_per_kv * g + hh]
                m = jnp.maximum(jnp.max(s, axis=-1, keepdims=True), sink)
                e = jnp.exp(s - m)
                den = jnp.sum(e, axis=-1, keepdims=True) + jnp.exp(sink - m)
                ps.append((e / den).astype(bf16))
            o_all = _dg(jnp.concatenate(ps, axis=0), vt)
            for cc in range(2):
                o = jnp.where(lane < HEAD_DIM,
                              o_all[2 * cc * WINDOW:(2 * cc + 1) * WINDOW, :],
                              o_all[(2 * cc + 1) * WINDOW:(2 * cc + 2) * WINDOW, :])
                c = 2 * g + cc
                abuf[pl.ds(r0, WINDOW), c * LANES:(c + 1) * LANES] = o
        return carry

    lax.fori_loop(0, ts // WINDOW, attn_block, 0, unroll=True)
    kbuf[0:WINDOW, :] = kbuf[ts:ts + WINDOW, :]
    vbuf[0:WINDOW, :] = vbuf[ts:ts + WINDOW, :]

    cb = proj(CB_OFF, CB_OFF + CONV_WIDTH)
    u = proj(CC_OFF, CC_OFF + CONV_WIDTH) * proj(CU_OFF, CU_OFF + CONV_WIDTH)
    rowi = lax.broadcasted_iota(jnp.int32, u.shape, 0)
    prev1 = ucarry[SUBLANES - 1:SUBLANES, :]
    prev2 = ucarry[SUBLANES - 2:SUBLANES - 1, :]
    u1 = jnp.where(rowi == 0, prev1, pltpu.roll(u, 1, 0))
    u2 = jnp.where(rowi == 0, prev2, jnp.where(rowi == 1, prev1, pltpu.roll(u, 2, 0)))
    ucarry[...] = u[ts - SUBLANES:ts, :]
    w = convw_ref[...]
    conv = cb * (w[0:1] * u2 + w[1:2] * u1 + w[2:3] * u)

    ra = (_rms(abuf[...]) * ga_ref[...]).astype(bf16)
    rc = (_rms(conv) * gc_ref[...]).astype(bf16)
    y = _dg(ra, wout_ref[0:ATTN_WIDTH, :]) + _dg(rc, wout_ref[ATTN_WIDTH:, :])
    x1 = x + gate1 * y
    x1_ref[...] = x1
    h2 = _rms(x1) * g2_ref[...]
    h2t_ref[...] = (h2 * (1.0 + scale2) + shift2).T.astype(bf16)


def _const_spec(shape):
    nd = len(shape)
    return pl.BlockSpec(shape, lambda *_: (0,) * nd, pipeline_mode=pl.Buffered(1))


def _mixer_call(sinks, x, pos, mod, g1, win, gq, gk, freq, sgn, msame, convw, ga, gc, wout, g2):
    B, S, D = x.shape
    ts = SEQ_TILE
    nst = S // ts
    return pl.pallas_call(
        _mixer_kernel,
        grid=(B, nst),
        in_specs=[
            pl.BlockSpec(memory_space=pltpu.SMEM),
            pl.BlockSpec((None, ts, D), lambda b, s: (b, s, 0)),
            pl.BlockSpec((None, ts, 1), lambda b, s: (b, s, 0)),
            pl.BlockSpec((None, N_MOD, D), lambda b, s: (b, 0, 0)),
            _const_spec(g1.shape), _const_spec(win.shape),
            _const_spec(gq.shape), _const_spec(gk.shape), _const_spec(freq.shape),
            _const_spec(sgn.shape), _const_spec(msame.shape), _const_spec(convw.shape),
            _const_spec(ga.shape), _const_spec(gc.shape), _const_spec(wout.shape),
            _const_spec(g2.shape),
        ],
        out_specs=[
            pl.BlockSpec((None, ts, D), lambda b, s: (b, s, 0)),
            pl.BlockSpec((D, ts), lambda b, s: (0, b * nst + s)),
        ],
        out_shape=[
            jax.ShapeDtypeStruct((B, S, D), f32),
            jax.ShapeDtypeStruct((D, B * S), bf16),
        ],
        scratch_shapes=[
            pltpu.VMEM((ts, ATTN_WIDTH), f32),
            pltpu.VMEM((WINDOW + ts, 2 * LANES), f32),
            pltpu.VMEM((WINDOW + ts, 2 * LANES), f32),
            pltpu.VMEM((ts, ATTN_WIDTH), f32),
            pltpu.VMEM((SUBLANES, CONV_WIDTH), f32),
        ],
        compiler_params=pltpu.CompilerParams(
            dimension_semantics=("arbitrary", "arbitrary"), vmem_limit_bytes=VMEM_LIMIT),
        name="mixer",
    )(sinks, x, pos, mod, g1, win, gq, gk, freq, sgn, msame, convw, ga, gc, wout, g2)


def _oddeven_merge_sort_pairs(n):
    pairs = []
    p = 1
    while p < n:
        k = p
        while k >= 1:
            for j in range(k % p, n - k, 2 * k):
                for i in range(min(k, n - j - k)):
                    if (i + j) // (2 * p) == (i + j + k) // (2 * p):
                        pairs.append((i + j, i + j + k))
            k //= 2
        p *= 2
    return pairs


_SORT16 = _oddeven_merge_sort_pairs(PEER_TOPK)


def _bitonic_desc(z):
    z = list(z)
    d = PEER_TOPK // 2
    while d >= 1:
        for r in range(PEER_TOPK):
            if not r & d:
                hi, lo = jnp.maximum(z[r], z[r + d]), jnp.minimum(z[r], z[r + d])
                z[r], z[r + d] = hi, lo
        d //= 2
    return z


def _merge_top(R, L):
    z = list(R)
    for r in range(PEER_TOPK - len(L), PEER_TOPK):
        z[r] = jnp.maximum(R[r], L[PEER_TOPK - 1 - r])
    return _bitonic_desc(z)


def _top16_rows(sc):
    x = [sc[SUBLANES * g:SUBLANES * (g + 1), :] for g in range(PEER_KEYS // SUBLANES)]
    for i, j in _SORT16:
        x[i], x[j] = jnp.maximum(x[i], x[j]), jnp.minimum(x[i], x[j])
    for shift in (4, 2, 1):
        z = [jnp.maximum(x[r], pltpu.roll(x[PEER_TOPK - 1 - r], shift, 0)) for r in range(PEER_TOPK)]
        x = _bitonic_desc(z)
    return x


def _prefix_count(test, rows):
    assert len(rows) == PEER_TOPK == 16
    sel = jnp.where
    p3 = test(rows[7])
    p2 = test(sel(p3, rows[11], rows[3]))
    p1 = test(sel(p3, sel(p2, rows[13], rows[9]), sel(p2, rows[5], rows[1])))
    hi = sel(p2, sel(p1, rows[14], rows[12]), sel(p1, rows[10], rows[8]))
    lo = sel(p2, sel(p1, rows[6], rows[4]), sel(p1, rows[2], rows[0]))
    p0 = test(sel(p3, hi, lo))
    bits = ((p3, 8.0), (p2, 4.0), (p1, 2.0), (p0, 1.0), (test(rows[15]), 1.0))
    return sum(sel(p, v, 0.0) for p, v in bits)


def _router_kernel(h2t_ref, wq_ref, sk_ref, rows_ref, gate_ref, q_s, sc_s, top_s, row_s):
    tr = h2t_ref.shape[1]
    q_s[...] = _dg(wq_ref[...], h2t_ref[...])
    top_s[...] = jnp.zeros(top_s.shape, f32)
    sub = lax.broadcasted_iota(jnp.int32, (SUBLANES, tr), 0)

    def head_body(h, carry):
        for p in range(2):
            hp = 2 * h + p
            r0 = pl.multiple_of(hp * PEER_KEYS, PEER_KEYS)
            sc = _dg(sk_ref[hp], q_s[pl.ds(r0, PEER_KEYS), :].astype(bf16))
            sc_s[hp] = sc
            top = _top16_rows(sc)
            for r in range(PEER_TOPK):
                top_s[p, r] = jnp.where(sub == h, top[r], top_s[p, r])
        return carry

    lax.fori_loop(0, PEER_HEADS, head_body, 0, unroll=4)

    a = [top_s[0, r] for r in range(PEER_TOPK)]
    b = [top_s[1, r] for r in range(PEER_TOPK)]
    R = [a[0] + b[r] for r in range(16)]
    R = _merge_top(R, [a[r] + b[0] for r in range(1, 16)])
    R = _merge_top(R, [a[1] + b[r] for r in range(1, 8)])
    R = _merge_top(R, [a[r] + b[1] for r in range(2, 8)])
    R = _merge_top(R, [a[2] + b[r] for r in range(2, 5)])
    R = _merge_top(R, [a[r] + b[2] for r in range(3, 5)])
    R = _merge_top(R, [a[3] + b[3]])
    tau = R[PEER_TOPK - 1]
    z = jnp.exp(R[0] - R[0])
    for r in range(1, PEER_TOPK):
        z = z + jnp.exp(R[r] - R[0])
    row_s[0] = a[0]
    row_s[1] = b[0]
    row_s[2] = 0.5 / z
    row_s[3] = tau

    def fac_body(h, carry):
        row = pl.ds(h, 1)
        s1 = sc_s[2 * h]
        s2 = sc_s[2 * h + 1]
        tau_h = row_s[3, row, :]
        b_rows = [top_s[1, r, row, :] for r in range(PEER_TOPK)]
        cnt = _prefix_count(lambda b_r: s1 + b_r >= tau_h, b_rows)
        rk = _prefix_count(lambda b_r: b_r > s2, b_rows)
        f1 = jnp.exp(s1 - row_s[0, row, :]) * row_s[2, row, :]
        f2 = jnp.exp(s2 - row_s[1, row, :])
        for tc in range(tr // LANES):
            cols = slice(tc * LANES, (tc + 1) * LANES)
            for ig in range(N_ROW_GROUPS):
                keys = slice(ig * SUBLANES, (ig + 1) * SUBLANES)
                rows_ref[tc, ig, h, 0] = f1[keys, cols]
                rows_ref[tc, ig, h, 1] = cnt[keys, cols]
            for v in range(N_KEY_VREGS):
                keys = slice(v * BF16_ROWS, (v + 1) * BF16_ROWS)
                gate_ref[tc, v, h, 0] = rk[keys, cols]
                gate_ref[tc, v, h, 1] = f2[keys, cols]
        return carry

    lax.fori_loop(0, PEER_HEADS, fac_body, 0)


def _router_call(h2t, wq, sk):
    D, T = h2t.shape
    tr = ROUTER_TILE
    rows_shape = (N_ROW_GROUPS, PEER_HEADS, 2, SUBLANES, LANES)
    gate_shape = (N_KEY_VREGS, PEER_HEADS, 2, BF16_ROWS, LANES)
    lead = lambda i: (i, 0, 0, 0, 0, 0)
    return pl.pallas_call(
        _router_kernel,
        grid=(T // tr,),
        in_specs=[
            pl.BlockSpec((D, tr), lambda i: (0, i)),
            _const_spec(wq.shape), _const_spec(sk.shape),
        ],
        out_specs=[
            pl.BlockSpec((tr // LANES,) + rows_shape, lead),
            pl.BlockSpec((tr // LANES,) + gate_shape, lead),
        ],
        out_shape=[
            jax.ShapeDtypeStruct((T // LANES,) + rows_shape, f32),
            jax.ShapeDtypeStruct((T // LANES,) + gate_shape, f32),
        ],
        scratch_shapes=[
            pltpu.VMEM((PEER_HEADS * PEER_QDIM, tr), f32),
            pltpu.VMEM((2 * PEER_HEADS, PEER_KEYS, tr), f32),
            pltpu.VMEM((2, PEER_TOPK, SUBLANES, tr), f32),
            pltpu.VMEM((4, SUBLANES, tr), f32),
        ],
        compiler_params=pltpu.CompilerParams(
            dimension_semantics=("arbitrary",), vmem_limit_bytes=VMEM_LIMIT),
        name="router",
    )(h2t, wq, sk)


_GELU_C1 = float(np.sqrt(2.0 / np.pi))
_GELU_C2 = 0.044715 * _GELU_C1


def _experts_kernel(h2b_ref, rows_ref, gate_ref, u_ref, vt_ref, x1_ref, mod_ref,
                    out_ref, acc, a_s, w_s, gate_s):
    tm = h2b_ref.shape[1]
    te = u_ref.shape[0]
    assert te % (SUBLANES * PEER_KEYS) == 0
    groups = te // (SUBLANES * PEER_KEYS)
    j = pl.program_id(1)

    @pl.when(j == 0)
    def _():
        acc[...] = jnp.zeros(acc.shape, f32)
        gate_s[...] = gate_ref[...].astype(bf16)

    a_s[:, 0:tm] = _dg(u_ref[...], h2b_ref[...])

    def packed_row(tc, h, kind, r):
        row = rows_ref[tc, j * groups + r // SUBLANES, h, kind][r % SUBLANES:r % SUBLANES + 1, :]
        return jnp.broadcast_to(row, (BF16_ROWS, LANES)).astype(bf16)

    for r in range(te // PEER_KEYS):
        for tc in range(tm // LANES):
            cols = slice(tc * LANES, (tc + 1) * LANES)
            f1 = [packed_row(tc, h, 0, r) for h in range(PEER_HEADS)]
            cnt = [packed_row(tc, h, 1, r) for h in range(PEER_HEADS)]
            for v in range(N_KEY_VREGS):
                rows = slice(r * PEER_KEYS + v * BF16_ROWS, r * PEER_KEYS + (v + 1) * BF16_ROWS)
                g = jnp.zeros((BF16_ROWS, LANES), bf16)
                for h in range(PEER_HEADS):
                    sel = jnp.where(gate_s[tc, v, h, 0] < cnt[h], gate_s[tc, v, h, 1],
                                    jnp.zeros((), bf16))
                    g = g + sel * f1[h]
                a = a_s[rows, cols]
                t = jnp.tanh(a * (_GELU_C1 + _GELU_C2 * (a * a)))
                w_s[rows, cols] = g * (a * (1.0 + t)).astype(bf16)

    acc[...] += _dg(vt_ref[...], w_s[:, 0:tm], _TN)

    @pl.when(j == pl.num_programs(1) - 1)
    def _():
        gate2 = mod_ref[N_MOD - 1:N_MOD, :]
        out_ref[...] = x1_ref[...] + gate2 * acc[...].T


def _experts_call(h2b, rows, gate, u_b, vt_b, x1, mod, tiles_per_batch):
    D, T = h2b.shape
    tm, te = TOK_TILE, EXP_TILE
    lead = lambda i, j: (i, 0, 0, 0, 0, 0)
    return pl.pallas_call(
        _experts_kernel,
        grid=(T // tm, PEER_EXPERTS // te),
        in_specs=[
            pl.BlockSpec((D, tm), lambda i, j: (0, i)),
            pl.BlockSpec((tm // LANES,) + rows.shape[1:], lead),
            pl.BlockSpec((tm // LANES,) + gate.shape[1:], lead),
            pl.BlockSpec((te, D), lambda i, j: (j, 0)),
            pl.BlockSpec((te, D), lambda i, j: (j, 0)),
            pl.BlockSpec((tm, D), lambda i, j: (i, 0)),
            pl.BlockSpec((None, N_MOD, D), lambda i, j: (i // tiles_per_batch, 0, 0)),
        ],
        out_specs=pl.BlockSpec((tm, D), lambda i, j: (i, 0)),
        out_shape=jax.ShapeDtypeStruct((T, D), f32),
        scratch_shapes=[
            pltpu.VMEM((D, tm), f32),
            pltpu.VMEM((te, tm + LANES), f32),
            pltpu.VMEM((te, tm + LANES), bf16),
            pltpu.VMEM((tm // LANES,) + gate.shape[1:], bf16),
        ],
        compiler_params=pltpu.CompilerParams(
            dimension_semantics=("arbitrary", "arbitrary"), vmem_limit_bytes=VMEM_LIMIT),
        name="experts",
    )(h2b, rows, gate, u_b, vt_b, x1, mod)


def _proj_columns():
    half = np.arange(HALF_DIM)
    cols = []
    for c in range(N_Q_HEADS // 2):
        for part in range(2):
            for head in (2 * c, 2 * c + 1):
                cols.append(head * HEAD_DIM + part * HALF_DIM + half)
    for g in range(N_KV_HEADS):
        for part in (0, 0, 1, 1):
            cols.append(ATTN_WIDTH + g * HEAD_DIM + part * HALF_DIM + half)
    for g in range(N_KV_HEADS):
        for _ in range(2):
            cols.append(ATTN_WIDTH + KV_WIDTH + g * HEAD_DIM + np.arange(HEAD_DIM))
    cols.append(np.arange(ATTN_WIDTH + 2 * KV_WIDTH, ATTN_WIDTH + 2 * KV_WIDTH + 3 * CONV_WIDTH))
    return np.concatenate(cols)


_PROJ_COLS = _proj_columns()
_TILE_DIM = np.concatenate([np.arange(HALF_DIM), np.arange(HALF_DIM),
                            HALF_DIM + np.arange(HALF_DIM), HALF_DIM + np.arange(HALF_DIM)])
_TILE_HEAD = (np.arange(LANES) // HALF_DIM) % 2
_MSAME = (_TILE_HEAD[:, None] == _TILE_HEAD[None, :]).astype(np.float32)


def kernel(x, c, positions, ada_w, ada_b, norm1_g, w_in, q_norm_g, k_norm_g, sinks, conv_w,
           attn_out_g, conv_out_g, w_out, norm2_g, peer_wq, peer_subkeys, peer_u, peer_v):
    B, S, D = x.shape
    T = B * S
    assert ada_w.shape[0] == 1, "single-layer block"
    assert D == D_MODEL and S % SEQ_TILE == 0 and T % TOK_TILE == 0 and T % ROUTER_TILE == 0
    assert S % TOK_TILE == 0

    c_pad = jnp.pad(c, ((0, SUBLANES - B), (0, 0)))
    mod = _mod_call(c_pad, ada_w[0], ada_b[0][None, :])[:B].reshape(B, N_MOD, D)

    inv_freq = ROPE_THETA ** (-jnp.arange(0, HEAD_DIM, 2, dtype=f32) / HEAD_DIM)
    freq = inv_freq[_TILE_DIM % HALF_DIM][None, :]
    sgn = jnp.asarray(np.where(np.arange(LANES) < LANES // 2, -1.0, 1.0), f32)[None, :]
    win = w_in[0][:, _PROJ_COLS].astype(bf16)
    wout = w_out[0].astype(bf16)
    gq = jnp.tile(q_norm_g[0][_TILE_DIM], ATTN_WIDTH // LANES)[None, :]
    gk = jnp.tile(k_norm_g[0][_TILE_DIM], N_KV_HEADS)[None, :]
    x1, h2t = _mixer_call(
        sinks[0], x, positions.astype(f32)[..., None], mod, norm1_g, win, gq, gk, freq, sgn,
        jnp.asarray(_MSAME, bf16), conv_w[0], attn_out_g, conv_out_g, wout, norm2_g)

    wq = peer_wq[0].T.astype(bf16)
    sk = peer_subkeys[0].reshape(2 * PEER_HEADS, PEER_KEYS, PEER_QDIM // 2).astype(bf16)
    rows, gate = _router_call(h2t, wq, sk)

    out = _experts_call(h2t, rows, gate, peer_u[0].astype(bf16), peer_v[0].astype(bf16),
                        x1.reshape(T, D), mod, S // TOK_TILE)
    return out.reshape(B, S, D)
```
